```python
import math
import jax
import jax.numpy as jnp
from jax import lax
import numpy as np

D_MODEL = 1024
BATCH = 2
SEQ = 8192
DEPTH = 1

GRID_W = 64
CTX_LEN = 256

DA_HEADS = 4
DA_HEAD_DIM = 64
DA_V_DIM = 2 * DA_HEAD_DIM
DA_WIDTH = DA_HEADS * DA_V_DIM
HG_HEADS = 4
HG_HEAD_DIM = 128
HG_WIDTH = HG_HEADS * HG_HEAD_DIM
MIX_WIDTH = DA_WIDTH + HG_WIDTH
IN_WIDTH = 3 * DA_WIDTH + 5 * HG_WIDTH
SPLITS = (DA_WIDTH, 2 * DA_WIDTH, 3 * DA_WIDTH, 3 * DA_WIDTH + HG_WIDTH,
          3 * DA_WIDTH + 2 * HG_WIDTH, 3 * DA_WIDTH + 3 * HG_WIDTH, 3 * DA_WIDTH + 4 * HG_WIDTH)
CTX_KV_END = 3 * DA_WIDTH + 3 * HG_WIDTH
CTX_SPLITS = (DA_WIDTH, 2 * DA_WIDTH, 2 * DA_WIDTH + HG_WIDTH, 2 * DA_WIDTH + 2 * HG_WIDTH)
CHUNK = 64
Q_BLOCK = 128
N_EXPERTS = 16
CAPACITY_FACTOR = 2
EXPERT_FF = 1024
ROPE_THETA = 10000.0
NORM_EPS = 1e-6

kernel_name = "hybrid_diffattn_hgrn2_ecmoe_dit_layer"


def rmsnorm(x, g):
    xf = x.astype(jnp.float32)
    y = xf * lax.rsqrt(jnp.mean(xf * xf, axis=-1, keepdims=True) + NORM_EPS)
    return (y * g.astype(jnp.float32)).astype(x.dtype)


def modulate(h, shift, scale):
    return h * (1.0 + scale) + shift


def axial_rope_tables(rows):
    half = DA_HEAD_DIM // 2
    inv_freq = 1.0 / (ROPE_THETA ** (jnp.arange(0, half, 2, dtype=jnp.float32) / half))
    r, col = jnp.meshgrid(jnp.arange(rows, dtype=jnp.float32),
                          jnp.arange(GRID_W, dtype=jnp.float32), indexing="ij")
    ang_r = r.reshape(-1)[:, None] * inv_freq
    ang_c = col.reshape(-1)[:, None] * inv_freq
    ang = jnp.concatenate([ang_r, ang_r, ang_c, ang_c], axis=-1)
    return jnp.cos(ang), jnp.sin(ang)


def apply_axial_rope(x, cos, sin):
    xf = x.astype(jnp.float32)
    x1, x2, x3, x4 = jnp.split(xf, 4, axis=-1)
    rot = jnp.concatenate([-x2, x1, -x4, x3], axis=-1)
    return (xf * cos + rot * sin).astype(x.dtype)


def qk_heads(a):
    b, t = a.shape[:2]
    return a.reshape(b, t, DA_HEADS, 2, DA_HEAD_DIM).transpose(0, 2, 3, 1, 4)


def to_heads(a, n_heads, dim):
    b, t = a.shape[:2]
    return a.reshape(b, t, n_heads, dim).transpose(0, 2, 1, 3)


def merge_heads(o):
    b, h, t, d = o.shape
    return o.transpose(0, 2, 1, 3).reshape(b, t, h * d)


def diff_attend(q, k, v, lam):
    s = jnp.einsum("bhiqd,bhikd->bhiqk", q, k).astype(jnp.float32) * (DA_HEAD_DIM ** -0.5)
    p = jax.nn.softmax(s, axis=-1)
    a = p[:, :, 0] - lam * p[:, :, 1]
    return jnp.einsum("bhqk,bhkd->bhqd", a.astype(v.dtype), v)


def hgrn2_chunk_scan(k, v, log_f, s0, q=None):
    b_, h_, t_, _ = k.shape
    dv = v.shape[-1]
    n = t_ // CHUNK

    def chunks(a):
        return jnp.moveaxis(a.astype(jnp.float32).reshape(b_, h_, n, CHUNK, a.shape[-1]), 2, 0)

    lower = jnp.tril(jnp.ones((CHUNK, CHUNK), dtype=bool))[:, :, None]

    def step(S, inp):
        kc, vc, lc = inp[0], inp[1], inp[2]
        bcum = jnp.cumsum(lc, axis=2)
        b_end = bcum[:, :, -1:, :]
        S_next = (jnp.exp(b_end[:, :, 0, :, None]) * S
                  + jnp.einsum("bhsk,bhsv->bhkv", kc * jnp.exp(b_end - bcum), vc))
        if q is None:
            return S_next, None
        qc = inp[3]
        rel = bcum[:, :, :, None, :] - bcum[:, :, None, :, :]
        decay = jnp.exp(jnp.where(lower, rel, -jnp.inf))
        scores = jnp.einsum("bhrk,bhsk,bhrsk->bhrs", qc, kc, decay)
        o = (jnp.einsum("bhrk,bhkv->bhrv", qc * jnp.exp(bcum), S)
             + jnp.einsum("bhrs,bhsv->bhrv", scores, vc))
        return S_next, o

    xs = (chunks(k), chunks(v), chunks(log_f))
    if q is not None:
        xs = xs + (chunks(q),)
    S_fin, o = lax.scan(step, s0, xs)
    if q is not None:
        o = jnp.moveaxis(o, 0, 2).reshape(b_, h_, t_, dv).astype(v.dtype)
    return o, S_fin


def forget_gate(z, lb):
    f = lb + (1.0 - lb) * jax.nn.sigmoid(z.astype(jnp.float32))
    return jnp.log(f), 1.0 - f


def token_mixer(h_lat, h_ctx, w_in, w_out, lam, lambda_init, da_subln, lb, hg_norm, cos, sin, ctx_out):
    b_ = h_lat.shape[0]
    da_q, da_k, da_v, hg_ff, hg_fb, hg_i, hg_q, hg_g = jnp.split(h_lat @ w_in, SPLITS, axis=-1)
    c_da_k, c_da_v, c_ff, c_fb, c_i = jnp.split(h_ctx @ w_in[:, DA_WIDTH:CTX_KV_END], CTX_SPLITS, axis=-1)
    if ctx_out:
        c_da_q = h_ctx @ w_in[:, :DA_WIDTH]
        c_hg_q, c_hg_g = jnp.split(h_ctx @ w_in[:, CTX_KV_END:], 2, axis=-1)

    q = apply_axial_rope(qk_heads(da_q), cos, sin)
    k_ctx = qk_heads(c_da_k)
    k_all = jnp.concatenate([k_ctx, apply_axial_rope(qk_heads(da_k), cos, sin)], axis=3)
    v_ctx = to_heads(c_da_v, DA_HEADS, DA_V_DIM)
    v_all = jnp.concatenate([v_ctx, to_heads(da_v, DA_HEADS, DA_V_DIM)], axis=2)
    t_ = q.shape[3]
    n_blk = t_ // Q_BLOCK
    q_blocks = jnp.moveaxis(q.reshape(b_, DA_HEADS, 2, n_blk, Q_BLOCK, DA_HEAD_DIM), 3, 0)
    o_blocks = lax.map(lambda qb: diff_attend(qb, k_all, v_all, lam), q_blocks)
    o_da = jnp.moveaxis(o_blocks, 0, 2).reshape(b_, DA_HEADS, t_, DA_V_DIM)
    out_da = merge_heads(rmsnorm(o_da, da_subln) * (1.0 - lambda_init))

    v_lat = to_heads(hg_i, HG_HEADS, HG_HEAD_DIM)
    v_c = to_heads(c_i, HG_HEADS, HG_HEAD_DIM)
    q_lat = to_heads(hg_q, HG_HEADS, HG_HEAD_DIM)
    q_c = to_heads(c_hg_q, HG_HEADS, HG_HEAD_DIM) if ctx_out else None
    s0 = jnp.zeros((b_, HG_HEADS, HG_HEAD_DIM, HG_HEAD_DIM), jnp.float32)
    o_hg_lat = None
    o_hg_ctx = None
    for d, (z_lat, z_ctx) in enumerate(((hg_ff, c_ff), (hg_fb, c_fb))):
        flip = (lambda a: a) if d == 0 else (lambda a: jnp.flip(a, axis=2))
        lf_l, k_l = forget_gate(z_lat, lb[d])
        lf_c, k_c = forget_gate(z_ctx, lb[d])
        k_l, lf_l = to_heads(k_l, HG_HEADS, HG_HEAD_DIM), to_heads(lf_l, HG_HEADS, HG_HEAD_DIM)
        k_c, lf_c = to_heads(k_c, HG_HEADS, HG_HEAD_DIM), to_heads(lf_c, HG_HEADS, HG_HEAD_DIM)
        o_c, S_c = hgrn2_chunk_scan(flip(k_c), flip(v_c), flip(lf_c), s0,
                                    q=None if q_c is None else flip(q_c))
        o_l, _ = hgrn2_chunk_scan(flip(k_l), flip(v_lat), flip(lf_l), S_c, q=flip(q_lat))
        o_hg_lat = flip(o_l) if o_hg_lat is None else o_hg_lat + flip(o_l)
        if ctx_out:
            o_hg_ctx = flip(o_c) if o_hg_ctx is None else o_hg_ctx + flip(o_c)
    out_hg = merge_heads(rmsnorm(o_hg_lat, hg_norm)) * jax.nn.silu(hg_g)

    y_lat = jnp.concatenate([out_da, out_hg], axis=-1) @ w_out
    if not ctx_out:
        return y_lat, None
    o_c_da = diff_attend(qk_heads(c_da_q), k_ctx, v_ctx, lam)
    out_c_da = merge_heads(rmsnorm(o_c_da, da_subln) * (1.0 - lambda_init))
    out_c_hg = merge_heads(rmsnorm(o_hg_ctx, hg_norm)) * jax.nn.silu(c_hg_g)
    y_ctx = jnp.concatenate([out_c_da, out_c_hg], axis=-1) @ w_out
    return y_lat, y_ctx


def expert_choice_ffn(h, w_router, w_gate, w_up, w_down):
    b_, t_, d_ = h.shape
    cap = CAPACITY_FACTOR * t_ // N_EXPERTS
    aff = jax.nn.softmax(jnp.einsum("btd,de->bte", h, w_router).astype(jnp.float32), axis=-1)
    top_aff, top_idx = lax.top_k(jnp.swapaxes(aff, 1, 2), cap)
    xin = jax.vmap(lambda hb, ib: hb[ib])(h, top_idx)
    g = jnp.einsum("becd,edf->becf", xin, w_gate)
    u = jnp.einsum("becd,edf->becf", xin, w_up)
    y = jnp.einsum("becf,efd->becd", jax.nn.silu(g) * u, w_down) * top_aff[..., None].astype(h.dtype)
    return jax.vmap(lambda yb, ib: jnp.zeros((t_, d_), h.dtype).at[ib.reshape(-1)].add(yb.reshape(-1, d_)))(y, top_idx)


def setup_inputs(seed: int = 0) -> dict:
    key = jax.random.key(seed)
    ks = jax.random.split(key, 24)
    D = D_MODEL

    def nrm(k, shape, scale):
        return jax.random.normal(k, shape, jnp.float32) * scale

    return {
        "x": nrm(ks[0], (BATCH, SEQ, D), 1.0),
        "c": nrm(ks[1], (BATCH, D), 1.0),
        "ctx": nrm(ks[2], (BATCH, CTX_LEN, D), 1.0),
        "c_ctx": nrm(ks[3], (D,), 1.0),
        "w_ada": nrm(ks[4], (DEPTH, D, 6 * D), 0.5 * D ** -0.5),
        "b_ada": nrm(ks[5], (DEPTH, 6 * D), 0.02),
        "norm_pre_mix": 1.0 + nrm(ks[6], (DEPTH, D), 0.05),
        "norm_post_mix": 1.0 + nrm(ks[7], (DEPTH, D), 0.05),
        "norm_pre_ffn": 1.0 + nrm(ks[8], (DEPTH, D), 0.05),
        "norm_post_ffn": 1.0 + nrm(ks[9], (DEPTH, D), 0.05),
        "w_in": nrm(ks[10], (DEPTH, D, IN_WIDTH), D ** -0.5),
        "da_lambda_q1": nrm(ks[11], (DEPTH, DA_HEAD_DIM), 0.1),
        "da_lambda_k1": nrm(ks[12], (DEPTH, DA_HEAD_DIM), 0.1),
        "da_lambda_q2": nrm(ks[13], (DEPTH, DA_HEAD_DIM), 0.1),
        "da_lambda_k2": nrm(ks[14], (DEPTH, DA_HEAD_DIM), 0.1),
        "da_subln": 1.0 + nrm(ks[15], (DEPTH, DA_V_DIM), 0.05),
        "hg_lower_bound": nrm(ks[16], (DEPTH + 1, 2, HG_WIDTH), 0.5),
        "hg_norm": 1.0 + nrm(ks[17], (DEPTH, HG_HEAD_DIM), 0.05),
        "w_out": nrm(ks[18], (DEPTH, MIX_WIDTH, D), MIX_WIDTH ** -0.5),
        "w_router": nrm(ks[19], (DEPTH, D, N_EXPERTS), D ** -0.5),
        "w_gate": nrm(ks[20], (DEPTH, N_EXPERTS, D, EXPERT_FF), D ** -0.5),
        "w_up": nrm(ks[21], (DEPTH, N_EXPERTS, D, EXPERT_FF), D ** -0.5),
        "w_down": nrm(ks[22], (DEPTH, N_EXPERTS, EXPERT_FF, D), EXPERT_FF ** -0.5),
    }


def reference(x, c, ctx, c_ctx, w_ada, b_ada, norm_pre_mix, norm_post_mix, norm_pre_ffn, norm_post_ffn,
              w_in, da_lambda_q1, da_lambda_k1, da_lambda_q2, da_lambda_k2, da_subln, hg_lower_bound,
              hg_norm, w_out, w_router, w_gate, w_up, w_down):
    T = x.shape[1]
    ROWS = T // GRID_W
    cos, sin = axial_rope_tables(ROWS)
    lb_all = jnp.cumsum(jax.nn.softmax(hg_lower_bound.astype(jnp.float32), axis=0), axis=0)
    x_lat, x_ctx = x, ctx
    for l in range(DEPTH):
        last = l == DEPTH - 1
        lambda_init = 0.8 - 0.6 * math.exp(-0.3 * l)
        lam = (jnp.exp(jnp.sum(da_lambda_q1[l].astype(jnp.float32) * da_lambda_k1[l].astype(jnp.float32)))
               - jnp.exp(jnp.sum(da_lambda_q2[l].astype(jnp.float32) * da_lambda_k2[l].astype(jnp.float32)))
               + lambda_init)
        mod_lat = jax.nn.silu(c) @ w_ada[l] + b_ada[l]
        mod_ctx = jax.nn.silu(c_ctx) @ w_ada[l] + b_ada[l]
        sh1, sc1, gt1, sh2, sc2, gt2 = jnp.split(mod_lat[:, None, :], 6, axis=-1)
        csh1, csc1, cgt1, csh2, csc2, cgt2 = jnp.split(mod_ctx, 6, axis=-1)

        h_lat = modulate(rmsnorm(x_lat, norm_pre_mix[l]), sh1, sc1)
        h_ctx = modulate(rmsnorm(x_ctx, norm_pre_mix[l]), csh1, csc1)
        y_lat, y_ctx = token_mixer(h_lat, h_ctx, w_in[l], w_out[l], lam, lambda_init, da_subln[l],
                                   lb_all[l], hg_norm[l], cos, sin, not last)
        x_lat = x_lat + gt1 * rmsnorm(y_lat, norm_post_mix[l])
        h_lat = modulate(rmsnorm(x_lat, norm_pre_ffn[l]), sh2, sc2)
        f_lat = expert_choice_ffn(h_lat, w_router[l], w_gate[l], w_up[l], w_down[l])
        x_lat = x_lat + gt2 * rmsnorm(f_lat, norm_post_ffn[l])
        if not last:
            x_ctx = x_ctx + cgt1 * rmsnorm(y_ctx, norm_post_mix[l])
            h_ctx = modulate(rmsnorm(x_ctx, norm_pre_ffn[l]), csh2, csc2)
            f_ctx = expert_choice_ffn(h_ctx, w_router[l], w_gate[l], w_up[l], w_down[l])
            x_ctx = x_ctx + cgt2 * rmsnorm(f_ctx, norm_post_ffn[l])
    return x_lat
```

```python
import functools
import math

import jax
import jax.numpy as jnp
import numpy as np
from jax import lax
from jax.experimental import pallas as pl
from jax.experimental.pallas import tpu as pltpu

F32 = jnp.float32
BF16 = jnp.bfloat16

GRID_W = 64
DA_HEADS = 4
DA_HEAD_DIM = 64
DA_V_DIM = 128
DA_WIDTH = 512
HG_HEADS = 4
HG_HEAD_DIM = 128
HG_WIDTH = 512
N_EXPERTS = 16
CAPACITY_FACTOR = 2
ROPE_THETA = 10000.0
NORM_EPS = 1e-6
LANES = 128
VMEM_LIMIT = 60 * 1024 * 1024


def _cparams(sem, vmem=VMEM_LIMIT):
    return pltpu.CompilerParams(dimension_semantics=sem, vmem_limit_bytes=vmem)


def _dot(a, b):
    return jnp.dot(a, b, preferred_element_type=F32)


def _dot_nt(a, b):
    return lax.dot_general(a, b, (((1,), (1,)), ((), ())), preferred_element_type=F32)


def _dot_tn(a, b):
    return lax.dot_general(a, b, (((0,), (0,)), ((), ())), preferred_element_type=F32)


def _split3(x):
    hi = x.astype(BF16)
    r1 = x - hi.astype(F32)
    mid = r1.astype(BF16)
    lo = (r1 - mid.astype(F32)).astype(BF16)
    return hi, mid, lo


def _mod_kernel(s_ref, w_ref, b_ref, o_ref):
    s = s_ref[...]
    s = s * jax.nn.sigmoid(s)
    hi, mid, lo = _split3(s)
    w = w_ref[...]
    whi, wmid, wlo = _split3(w)
    acc = _dot(hi, whi) + (_dot(hi, wmid) + _dot(mid, whi)) + (_dot(hi, wlo) + _dot(mid, wmid) + _dot(lo, whi))
    o_ref[...] = acc + b_ref[...]


def _mod_call(cmat, w_ada, b_ada):
    d = cmat.shape[1]
    n = w_ada.shape[1]
    tn = 1536 if n % 1536 == 0 else n
    return pl.pallas_call(
        _mod_kernel,
        grid=(n // tn,),
        in_specs=[pl.BlockSpec((8, d), lambda j: (0, 0)),
                  pl.BlockSpec((d, tn), lambda j: (0, j)),
                  pl.BlockSpec((1, tn), lambda j: (0, j))],
        out_specs=pl.BlockSpec((8, tn), lambda j: (0, j)),
        out_shape=jax.ShapeDtypeStruct((8, n), F32),
        compiler_params=_cparams(("arbitrary",)),
        name="mod",
    )(cmat, w_ada, b_ada)


def _inproj_kernel(x_ref, ctx_ref, mod_ref, g_ref, w_ref, cos_ref, sin_ref, qkv_ref, z_ref, iqg_ref, *, n_lat, d):
    b = pl.program_id(0)
    i = pl.program_id(1)
    is_ctx = i == n_lat
    xb = jnp.where(is_ctx, ctx_ref[0], x_ref[0])
    row = jnp.where(is_ctx, 2, b)
    sh = mod_ref[pl.ds(row, 1), 0:d]
    sc = mod_ref[pl.ds(row, 1), d:2 * d]
    ms = jnp.mean(xb * xb, axis=-1, keepdims=True)
    h = xb * lax.rsqrt(ms + NORM_EPS) * g_ref[...]
    h = h * (1.0 + sc) + sh
    p = _dot(h.astype(BF16), w_ref[...])
    cos = jnp.tile(cos_ref[...], (1, 8))
    sin = jnp.tile(sin_ref[...], (1, 8))
    qk = p[:, 0:1024] * cos + p[:, 4096:5120] * sin
    qkv_ref[0, :, 0:1024] = qk.astype(BF16)
    qkv_ref[0, :, 1024:1536] = p[:, 1024:1536].astype(BF16)
    z_ref[0] = p[:, 1536:2560]
    iqg_ref[0] = p[:, 2560:4096].astype(BF16)


def _inproj_call(x, ctx, mod, g, w, cos, sin):
    bsz, t, d = x.shape
    tm = ctx.shape[1]
    n_lat = t // tm
    rows = t + tm
    kern = functools.partial(_inproj_kernel, n_lat=n_lat, d=d)
    return pl.pallas_call(
        kern,
        grid=(bsz, n_lat + 1),
        in_specs=[pl.BlockSpec((1, tm, d), lambda b, i: (b, jnp.minimum(i, n_lat - 1), 0)),
                  pl.BlockSpec((1, tm, d), lambda b, i: (b, 0, 0)),
                  pl.BlockSpec(mod.shape, lambda b, i: (0, 0)),
                  pl.BlockSpec((1, d), lambda b, i: (0, 0)),
                  pl.BlockSpec(w.shape, lambda b, i: (0, 0)),
                  pl.BlockSpec((tm, LANES), lambda b, i: (i, 0)),
                  pl.BlockSpec((tm, LANES), lambda b, i: (i, 0))],
        out_specs=[pl.BlockSpec((1, tm, 1536), lambda b, i: (b, i, 0)),
                   pl.BlockSpec((1, tm, 1024), lambda b, i: (b, i, 0)),
                   pl.BlockSpec((1, tm, 1536), lambda b, i: (b, i, 0))],
        out_shape=[jax.ShapeDtypeStruct((bsz, rows, 1536), BF16),
                   jax.ShapeDtypeStruct((bsz, rows, 1024), F32),
                   jax.ShapeDtypeStruct((bsz, rows, 1536), BF16)],
        compiler_params=_cparams(("parallel", "arbitrary")),
        name="inproj",
    )(x, ctx, mod, g, w, cos, sin)


def _attn_kernel(lam_ref, q_ref, k_ref, v_ref, subln_ref, o_ref, qs_ref, m_ref, l_ref, acc_ref, *, tq, out_scale):
    kj = pl.program_id(3)

    @pl.when(kj == 0)
    def _():
        q = q_ref[0]
        lane = lax.broadcasted_iota(jnp.int32, q.shape, 1)
        zero = jnp.zeros_like(q)
        qs_ref[0:tq, :] = jnp.where(lane < DA_HEAD_DIM, q, zero)
        qs_ref[tq:2 * tq, :] = jnp.where(lane >= DA_HEAD_DIM, q, zero)
        m_ref[...] = jnp.full(m_ref.shape, -1e30, F32)
        l_ref[...] = jnp.zeros(l_ref.shape, F32)
        acc_ref[...] = jnp.zeros(acc_ref.shape, F32)

    s = _dot_nt(qs_ref[...], k_ref[0])
    m_prev = m_ref[...]
    m_new = jnp.maximum(m_prev, jnp.max(s, axis=-1, keepdims=True))
    alpha = jnp.exp(m_prev - m_new)
    p = jnp.exp(s - m_new)
    l_ref[...] = alpha * l_ref[...] + jnp.sum(p, axis=-1, keepdims=True)
    acc_ref[...] = alpha * acc_ref[...] + _dot(p.astype(BF16), v_ref[0])
    m_ref[...] = m_new

    @pl.when(kj == pl.num_programs(3) - 1)
    def _():
        r = 1.0 / l_ref[...]
        acc = acc_ref[...]
        o = acc[0:tq] * r[0:tq] - lam_ref[0, 0] * (acc[tq:2 * tq] * r[tq:2 * tq])
        ms = jnp.mean(o * o, axis=-1, keepdims=True)
        o = o * lax.rsqrt(ms + NORM_EPS) * subln_ref[...] * out_scale
        o_ref[0] = o.astype(o_ref.dtype)


def _attn_call(lam, qkv, subln, t, tq, tk, out_scale):
    bsz, rows, _ = qkv.shape
    kern = functools.partial(_attn_kernel, tq=tq, out_scale=out_scale)
    return pl.pallas_call(
        kern,
        grid=(bsz, DA_HEADS, t // tq, rows // tk),
        in_specs=[pl.BlockSpec(memory_space=pltpu.SMEM),
                  pl.BlockSpec((1, tq, LANES), lambda b, h, i, j: (b, i, h)),
                  pl.BlockSpec((1, tk, LANES), lambda b, h, i, j: (b, j, DA_HEADS + h)),
                  pl.BlockSpec((1, tk, LANES), lambda b, h, i, j: (b, j, 2 * DA_HEADS + h)),
                  pl.BlockSpec((1, LANES), lambda b, h, i, j: (0, 0))],
        out_specs=pl.BlockSpec((1, tq, LANES), lambda b, h, i, j: (b, i, h)),
        out_shape=jax.ShapeDtypeStruct((bsz, t, DA_WIDTH), BF16),
        scratch_shapes=[pltpu.VMEM((2 * tq, LANES), BF16),
                        pltpu.VMEM((2 * tq, 1), F32),
                        pltpu.VMEM((2 * tq, 1), F32),
                        pltpu.VMEM((2 * tq, LANES), F32)],
        compiler_params=_cparams(("parallel", "parallel", "parallel", "arbitrary")),
        name="attn",
    )(lam, qkv, qkv, qkv, subln)


def _hgrn_consts(c):
    nl = int(math.log2(c))
    t = np.arange(c)
    lm = np.zeros((2, (nl + 2) * c, c), np.float32)
    for d in range(2):
        pi = t if d == 0 else c - 1 - t
        pr, pu = pi[:, None], pi[None, :]
        for lg in range(nl):
            same = (pr >> lg) == (pu >> lg)
            sec = ((pr >> lg) & 1) == 1
            m = same & np.where(sec, pu <= pr, pu > pr)
            lm[d, lg * c:(lg + 1) * c] = m
        lm[d, nl * c:(nl + 1) * c] = pu <= pr
        lm[d, (nl + 1) * c:(nl + 2) * c] = pu > pr
    masks = np.zeros((nl + 1, c, c), np.float32)
    for lg in range(nl):
        masks[lg] = (t[:, None] >> (lg + 1)) == (t[None, :] >> (lg + 1))
    masks[nl] = np.eye(c)
    return jnp.asarray(lm, BF16), jnp.asarray(masks, F32), nl


def _hgrn_kernel(z_ref, i_ref, q_ref, lb_ref, lmat_ref, mask_ref, o_ref, st_ref, *, c, nl):
    d = pl.program_id(2)
    s = pl.program_id(3)

    @pl.when(s == 0)
    def _():
        st_ref[...] = jnp.zeros(st_ref.shape, F32)

    z = z_ref[0]
    lb = lb_ref[0, 0]
    f = lb + (1.0 - lb) * jax.nn.sigmoid(z)
    lf = jnp.log(f)
    kk = 1.0 - f
    v = i_ref[0]
    q = q_ref[0].astype(F32)
    hi, mid, lo = _split3(lf)
    x3 = _dot(lmat_ref[0], jnp.concatenate([hi, mid, lo], axis=-1))
    x = x3[:, 0:LANES] + x3[:, LANES:2 * LANES] + x3[:, 2 * LANES:3 * LANES]

    row = lax.broadcasted_iota(jnp.int32, (c, LANES), 0)
    pi = jnp.where(d == 0, row, c - 1 - row)
    scores = mask_ref[nl] * _dot_nt(q.astype(BF16), kk.astype(BF16))
    for lg in range(nl):
        g = jnp.exp(x[lg * c:(lg + 1) * c])
        sec = ((pi >> lg) & 1) == 1
        qt = jnp.where(sec, q * g, 0.0).astype(BF16)
        kt = jnp.where(sec, 0.0, kk * g).astype(BF16)
        scores = scores + mask_ref[lg] * _dot_nt(qt, kt)
    bpre = x[nl * c:(nl + 1) * c]
    bsuf = x[(nl + 1) * c:(nl + 2) * c]
    bend = jnp.where(d == 0, bpre[c - 1:c], bpre[0:1])
    st = st_ref[...]
    o = _dot_nt((q * jnp.exp(bpre)).astype(BF16), st.astype(BF16)) + _dot(scores.astype(BF16), v)
    kb = (kk * jnp.exp(bsuf)).astype(BF16)
    st_ref[...] = st * jnp.exp(bend) + _dot_tn(v, kb)
    o_ref[0, 0] = o.astype(o_ref.dtype)


def _hgrn_call(z, iqg, lb4, c):
    bsz, rows, _ = z.shape
    nb = rows // c - 1
    lmat, masks, nl = _hgrn_consts(c)

    def blk(d, s):
        return jnp.where(s == 0, nb, jnp.where(d == 0, s - 1, nb - s))

    kern = functools.partial(_hgrn_kernel, c=c, nl=nl)
    return pl.pallas_call(
        kern,
        grid=(bsz, HG_HEADS, 2, nb + 1),
        in_specs=[pl.BlockSpec((1, c, LANES), lambda b, h, d, s: (b, blk(d, s), d * HG_HEADS + h)),
                  pl.BlockSpec((1, c, LANES), lambda b, h, d, s: (b, blk(d, s), h)),
                  pl.BlockSpec((1, c, LANES), lambda b, h, d, s: (b, blk(d, s), HG_HEADS + h)),
                  pl.BlockSpec((1, 1, 1, LANES), lambda b, h, d, s: (d, h, 0, 0)),
                  pl.BlockSpec((1,) + lmat.shape[1:], lambda b, h, d, s: (d, 0, 0)),
                  pl.BlockSpec(masks.shape, lambda b, h, d, s: (0, 0, 0))],
        out_specs=pl.BlockSpec((1, 1, c, LANES), lambda b, h, d, s: (b, d, blk(d, s), h)),
        out_shape=jax.ShapeDtypeStruct((bsz, 2, rows, HG_WIDTH), BF16),
        scratch_shapes=[pltpu.VMEM((HG_HEAD_DIM, HG_HEAD_DIM), F32)],
        compiler_params=_cparams(("parallel", "parallel", "parallel", "arbitrary")),
        name="hgrn",
    )(z, iqg, iqg, lb4, lmat, masks)


def _post_kernel(da_ref, of_ref, ob_ref, g_ref, x_ref, mod_ref, hgn_ref, wout_ref, npost_ref, npre_ref,
                 wrh_ref, wrl_ref, x1_ref, h2_ref, aff_ref, *, d):
    b = pl.program_id(0)
    o = of_ref[0, 0].astype(F32) + ob_ref[0, 0].astype(F32)
    parts = []
    for h in range(HG_HEADS):
        oh = o[:, h * LANES:(h + 1) * LANES]
        ms = jnp.mean(oh * oh, axis=-1, keepdims=True)
        parts.append(oh * lax.rsqrt(ms + NORM_EPS) * hgn_ref[...])
    g = g_ref[0].astype(F32)
    ohn = jnp.concatenate(parts, axis=-1) * (g * jax.nn.sigmoid(g))
    mixed = jnp.concatenate([da_ref[0], ohn.astype(BF16)], axis=-1)
    y = _dot(mixed, wout_ref[...])
    yn = y * lax.rsqrt(jnp.mean(y * y, axis=-1, keepdims=True) + NORM_EPS) * npost_ref[...]
    gt1 = mod_ref[pl.ds(b, 1), 2 * d:3 * d]
    sh2 = mod_ref[pl.ds(b, 1), 3 * d:4 * d]
    sc2 = mod_ref[pl.ds(b, 1), 4 * d:5 * d]
    x1 = x_ref[0] + gt1 * yn
    x1_ref[0] = x1
    h2 = x1 * lax.rsqrt(jnp.mean(x1 * x1, axis=-1, keepdims=True) + NORM_EPS) * npre_ref[...]
    h2 = h2 * (1.0 + sc2) + sh2
    h2_ref[0] = h2
    hh = h2.astype(BF16)
    hl = (h2 - hh.astype(F32)).astype(BF16)
    logits = _dot(hh, wrh_ref[...]) + (_dot(hl, wrh_ref[...]) + _dot(hh, wrl_ref[...]))
    mx = jnp.max(logits, axis=-1, keepdims=True)
    e = jnp.exp(logits - mx)
    aff_ref[0] = e / jnp.sum(e, axis=-1, keepdims=True)


def _post_call(out_da, o_hg, iqg, x, mod, hg_norm, w_out, npost, npre, wr_hi, wr_lo, tm):
    bsz, t, d = x.shape
    kern = functools.partial(_post_kernel, d=d)
    row = lambda b, i: (0, 0)
    return pl.pallas_call(
        kern,
        grid=(bsz, t // tm),
        in_specs=[pl.BlockSpec((1, tm, DA_WIDTH), lambda b, i: (b, i, 0)),
                  pl.BlockSpec((1, 1, tm, HG_WIDTH), lambda b, i: (b, 0, i, 0)),
                  pl.BlockSpec((1, 1, tm, HG_WIDTH), lambda b, i: (b, 1, i, 0)),
                  pl.BlockSpec((1, tm, HG_WIDTH), lambda b, i: (b, i, 2)),
                  pl.BlockSpec((1, tm, d), lambda b, i: (b, i, 0)),
                  pl.BlockSpec(mod.shape, row),
                  pl.BlockSpec((1, LANES), row),
                  pl.BlockSpec(w_out.shape, row),
                  pl.BlockSpec((1, d), row),
                  pl.BlockSpec((1, d), row),
                  pl.BlockSpec(wr_hi.shape, row),
                  pl.BlockSpec(wr_lo.shape, row)],
        out_specs=[pl.BlockSpec((1, tm, d), lambda b, i: (b, i, 0)),
                   pl.BlockSpec((1, tm, d), lambda b, i: (b, i, 0)),
                   pl.BlockSpec((1, tm, N_EXPERTS), lambda b, i: (b, i, 0))],
        out_shape=[jax.ShapeDtypeStruct((bsz, t, d), F32),
                   jax.ShapeDtypeStruct((bsz, t, d), F32),
                   jax.ShapeDtypeStruct((bsz, t, N_EXPERTS), F32)],
        compiler_params=_cparams(("parallel", "arbitrary")),
        name="post",
    )(out_da, o_hg, o_hg, iqg, x, mod, hg_norm, w_out, npost, npre, wr_hi, wr_lo)


def _topk_kernel(aff_ref, ltri_ref, idx_ref, gate_ref, selc_ref, *, t, cap, rb):
    ne = N_EXPERTS
    nblk = t // rb

    def as_float(bits):
        return lax.bitcast_convert_type(bits, F32)

    def count_ge(thr):
        def body(i, acc):
            a = aff_ref[0, pl.ds(i * rb, rb), :]
            return acc + jnp.sum(jnp.where(a >= thr, 1, 0), axis=0, keepdims=True)
        return lax.fori_loop(0, nblk, body, jnp.zeros((1, ne), jnp.int32))

    def bit_body(k, tau):
        cand = tau | (jnp.int32(1) << (29 - k))
        return jnp.where(count_ge(as_float(cand)) >= cap, cand, tau)
    tau = lax.fori_loop(0, 30, bit_body, jnp.zeros((1, ne), jnp.int32))
    tau_f = as_float(tau)
    nxt_f = as_float(tau + 1)

    n_gt = count_ge(nxt_f)
    need = (cap - n_gt).astype(F32)

    ltri = ltri_ref[...]

    def sel_body(i, carry):
        ceq, csel = carry
        a = aff_ref[0, pl.ds(i * rb, rb), :]
        gt = a >= nxt_f
        eq = (a >= tau_f) & jnp.logical_not(gt)
        eq_rank = _dot(ltri, jnp.where(eq, 1.0, 0.0).astype(BF16)) + ceq
        sel = gt | (eq & (eq_rank <= need))
        sel_f = jnp.where(sel, 1.0, 0.0)
        c_incl = _dot(ltri, sel_f.astype(BF16)) + csel
        selc_ref[pl.ds(i * rb, rb), :] = jnp.where(sel, c_incl, 0.0)
        return eq_rank[rb - 1:rb], c_incl[rb - 1:rb]
    zero = jnp.zeros((1, ne), F32)
    lax.fori_loop(0, nblk, sel_body, (zero, zero))

    n_pt = cap // LANES
    lane = lax.broadcasted_iota(jnp.int32, (1, LANES), 1).astype(F32)
    for e in range(ne):
        def slot_body(i, carry):
            cc = selc_ref[pl.ds(i * rb, rb), e:e + 1]
            aa = aff_ref[0, pl.ds(i * rb, rb), e:e + 1]
            tt = (lax.broadcasted_iota(jnp.int32, (rb, 1), 0) + i * rb).astype(F32)
            new = []
            for j in range(n_pt):
                pv = lane + (j * LANES + 1.0)
                hit = cc == pv
                ti = jnp.sum(jnp.where(hit, tt, 0.0).reshape(rb // 8, 8, LANES), axis=0)
                gi = jnp.sum(jnp.where(hit, aa, 0.0).reshape(rb // 8, 8, LANES), axis=0)
                new.append((carry[j][0] + ti, carry[j][1] + gi))
            return tuple(new)
        z8 = jnp.zeros((8, LANES), F32)
        acc = lax.fori_loop(0, nblk, slot_body, tuple((z8, z8) for _ in range(n_pt)))
        for j in range(n_pt):
            idx_ref[0, e:e + 1, j * LANES:(j + 1) * LANES] = (
                jnp.sum(acc[j][0], axis=0, keepdims=True).astype(jnp.int32))
            gate_ref[0, e:e + 1, j * LANES:(j + 1) * LANES] = jnp.sum(acc[j][1], axis=0, keepdims=True)


def _topk_call(aff, cap):
    bsz, t, ne = aff.shape
    rb = 256
    ltri = jnp.asarray(np.tril(np.ones((rb, rb), np.float32)), BF16)
    kern = functools.partial(_topk_kernel, t=t, cap=cap, rb=rb)
    return pl.pallas_call(
        kern,
        grid=(bsz,),
        in_specs=[pl.BlockSpec((1, t, ne), lambda b: (b, 0, 0)),
                  pl.BlockSpec((rb, rb), lambda b: (0, 0))],
        out_specs=[pl.BlockSpec((1, ne, cap), lambda b: (b, 0, 0)),
                   pl.BlockSpec((1, ne, cap), lambda b: (b, 0, 0))],
        out_shape=[jax.ShapeDtypeStruct((bsz, ne, cap), jnp.int32),
                   jax.ShapeDtypeStruct((bsz, ne, cap), F32)],
        scratch_shapes=[pltpu.VMEM((t, ne), F32)],
        compiler_params=_cparams(("arbitrary",)),
        name="topk",
    )(aff, ltri)


def _moe_kernel(idx_ref, h2_ref, gate_ref, wg_ref, wu_ref, wd_ref, f_ref,
                f_scr, xin, xbf, yacc, gsem, osem, *, t, cap, ne, nf, zr):
    b = pl.program_id(0)
    e = pl.program_id(1)
    fc = pl.program_id(2)
    base = (b * ne + e) * cap

    @pl.when((e == 0) & (fc == 0))
    def _():
        def zbody(i, _):
            f_scr[pl.ds(i * zr, zr), :] = jnp.zeros((zr, f_scr.shape[1]), F32)
            return 0
        lax.fori_loop(0, t // zr, zbody, 0)

    def row_copy(p):
        tok = idx_ref[base + p]
        return pltpu.make_async_copy(h2_ref.at[b, pl.ds(tok, 1), :], xin.at[pl.ds(p, 1), :], gsem)

    @pl.when(fc == 0)
    def _():
        def gstart(p, _):
            row_copy(p).start()
            return 0
        lax.fori_loop(0, cap, gstart, 0)

        def gwait(p, _):
            row_copy(p).wait()
            return 0
        lax.fori_loop(0, cap, gwait, 0)
        xbf[...] = xin[...].astype(BF16)

    x = xbf[...]
    g = _dot(x, wg_ref[0].astype(BF16))
    u = _dot(x, wu_ref[0].astype(BF16))
    hmid = (g * jax.nn.sigmoid(g) * u).astype(BF16)
    contrib = _dot(hmid, wd_ref[0].astype(BF16))

    @pl.when(fc == 0)
    def _():
        yacc[...] = contrib

    @pl.when(fc != 0)
    def _():
        yacc[...] = yacc[...] + contrib

    @pl.when(fc == nf - 1)
    def _():
        yacc[...] = yacc[...] * gate_ref[0, 0]

        def sbody(p, _):
            tok = idx_ref[base + p]
            f_scr[pl.ds(tok, 1), :] = f_scr[pl.ds(tok, 1), :] + yacc[pl.ds(p, 1), :]
            return 0
        lax.fori_loop(0, cap, sbody, 0)

    @pl.when((e == ne - 1) & (fc == nf - 1))
    def _():
        cp = pltpu.make_async_copy(f_scr, f_ref.at[b], osem)
        cp.start()
        cp.wait()


def _moe_call(idx_flat, h2, gate4, w_gate, w_up, w_down, cap, tf):
    bsz, t, d = h2.shape
    ne, _, ff = w_gate.shape
    nf = ff // tf
    kern = functools.partial(_moe_kernel, t=t, cap=cap, ne=ne, nf=nf, zr=256)
    grid_spec = pltpu.PrefetchScalarGridSpec(
        num_scalar_prefetch=1,
        grid=(bsz, ne, nf),
        in_specs=[pl.BlockSpec(memory_space=pl.ANY),
                  pl.BlockSpec((1, 1, cap, 1), lambda b, e, f, idx: (b, e, 0, 0)),
                  pl.BlockSpec((1, d, tf), lambda b, e, f, idx: (e, 0, f)),
                  pl.BlockSpec((1, d, tf), lambda b, e, f, idx: (e, 0, f)),
                  pl.BlockSpec((1, tf, d), lambda b, e, f, idx: (e, f, 0))],
        out_specs=pl.BlockSpec(memory_space=pl.ANY),
        scratch_shapes=[pltpu.VMEM((t, d), F32),
                        pltpu.VMEM((cap, d), F32),
                        pltpu.VMEM((cap, d), BF16),
                        pltpu.VMEM((cap, d), F32),
                        pltpu.SemaphoreType.DMA(()),
                        pltpu.SemaphoreType.DMA(())],
    )
    return pl.pallas_call(
        kern,
        grid_spec=grid_spec,
        out_shape=jax.ShapeDtypeStruct((bsz, t, d), F32),
        compiler_params=_cparams(("arbitrary", "arbitrary", "arbitrary")),
        name="moe",
    )(idx_flat, h2, gate4, w_gate, w_up, w_down)


def _final_kernel(x1_ref, f_ref, mod_ref, n_ref, o_ref, *, d):
    b = pl.program_id(0)
    f = f_ref[0]
    fn = f * lax.rsqrt(jnp.mean(f * f, axis=-1, keepdims=True) + NORM_EPS) * n_ref[...]
    gt2 = mod_ref[pl.ds(b, 1), 5 * d:6 * d]
    o_ref[0] = x1_ref[0] + gt2 * fn


def _final_call(x1, f, mod, npost, tm):
    bsz, t, d = x1.shape
    kern = functools.partial(_final_kernel, d=d)
    return pl.pallas_call(
        kern,
        grid=(bsz, t // tm),
        in_specs=[pl.BlockSpec((1, tm, d), lambda b, i: (b, i, 0)),
                  pl.BlockSpec((1, tm, d), lambda b, i: (b, i, 0)),
                  pl.BlockSpec(mod.shape, lambda b, i: (0, 0)),
                  pl.BlockSpec((1, d), lambda b, i: (0, 0))],
        out_specs=pl.BlockSpec((1, tm, d), lambda b, i: (b, i, 0)),
        out_shape=jax.ShapeDtypeStruct((bsz, t, d), F32),
        compiler_params=_cparams(("parallel", "arbitrary")),
        name="final",
    )(x1, f, mod, npost)


def _rope_tables(t, ctx_len):
    half = DA_HEAD_DIM // 2
    inv_freq = 1.0 / (ROPE_THETA ** (jnp.arange(0, half, 2, dtype=F32) / half))
    pos = jnp.arange(t, dtype=jnp.int32)
    r = (pos // GRID_W).astype(F32)
    col = (pos % GRID_W).astype(F32)
    ang_r = r[:, None] * inv_freq
    ang_c = col[:, None] * inv_freq
    ang = jnp.concatenate([ang_r, ang_r, ang_c, ang_c], axis=-1)
    cos = jnp.concatenate([jnp.cos(ang), jnp.ones((ctx_len, DA_HEAD_DIM), F32)], axis=0)
    sin = jnp.concatenate([jnp.sin(ang), jnp.zeros((ctx_len, DA_HEAD_DIM), F32)], axis=0)
    return jnp.tile(cos, (1, 2)), jnp.tile(sin, (1, 2))


def _rot_columns(w):
    k, n = w.shape
    w4 = w.reshape(k, n // DA_HEAD_DIM, 4, DA_HEAD_DIM // 4)
    return jnp.stack([-w4[:, :, 1], w4[:, :, 0], -w4[:, :, 3], w4[:, :, 2]], axis=2).reshape(k, n)


def kernel(x, c, ctx, c_ctx, w_ada, b_ada, norm_pre_mix, norm_post_mix, norm_pre_ffn, norm_post_ffn, w_in,
           da_lambda_q1, da_lambda_k1, da_lambda_q2, da_lambda_k2, da_subln, hg_lower_bound, hg_norm, w_out,
           w_router, w_gate, w_up, w_down):
    bsz, t, d = x.shape
    ctx_len = ctx.shape[1]
    assert bsz <= 2 and d == 1024 and w_ada.shape[0] == 1
    cap = CAPACITY_FACTOR * t // N_EXPERTS
    lambda_init = 0.8 - 0.6 * math.exp(-0.3 * 0)

    lam = (jnp.exp(jnp.sum(da_lambda_q1[0].astype(F32) * da_lambda_k1[0].astype(F32)))
           - jnp.exp(jnp.sum(da_lambda_q2[0].astype(F32) * da_lambda_k2[0].astype(F32))) + lambda_init)
    lam = lam.reshape(1, 1)
    lb = jnp.cumsum(jax.nn.softmax(hg_lower_bound.astype(F32), axis=0), axis=0)[0]
    lb4 = lb.reshape(2, HG_HEADS, 1, HG_HEAD_DIM)

    wi = w_in[0]
    scale = DA_HEAD_DIM ** -0.5
    wq = wi[:, 0:DA_WIDTH] * scale
    wk = wi[:, DA_WIDTH:2 * DA_WIDTH]
    w_all = jnp.concatenate([wq, wk, wi[:, 2 * DA_WIDTH:], _rot_columns(wq), _rot_columns(wk)],
                            axis=1).astype(BF16)
    cos, sin = _rope_tables(t, ctx_len)

    cmat = jnp.concatenate([c, c_ctx[None, :], jnp.zeros((8 - bsz - 1, d), F32)], axis=0)
    mod = _mod_call(cmat, w_ada[0], b_ada[0][None, :])

    qkv, z, iqg = _inproj_call(x, ctx, mod, norm_pre_mix[0][None, :], w_all, cos, sin)

    tq = min(512, t)
    nk = t + ctx_len
    tk = 768 if nk % 768 == 0 else ctx_len
    out_da = _attn_call(lam, qkv, da_subln[0][None, :], t, tq, tk, 1.0 - lambda_init)
    o_hg = _hgrn_call(z, iqg, lb4, ctx_len)

    wr = w_router[0]
    wr_hi = wr.astype(BF16)
    wr_lo = (wr - wr_hi.astype(F32)).astype(BF16)
    x1, h2, aff = _post_call(out_da, o_hg, iqg, x, mod, hg_norm[0][None, :], w_out[0].astype(BF16),
                             norm_post_mix[0][None, :], norm_pre_ffn[0][None, :], wr_hi, wr_lo, 256)
    idx, gate = _topk_call(aff, cap)
    f = _moe_call(idx.reshape(-1), h2, gate.reshape(bsz, N_EXPERTS, cap, 1), w_gate[0], w_up[0], w_down[0],
                  cap, 256)
    return _final_call(x1, f, mod, norm_post_ffn[0][None, :], 256)
```

```python
import functools
import math

import jax
import jax.numpy as jnp
import numpy as np
from jax import lax
from jax.experimental import pallas as pl
from jax.experimental.pallas import tpu as pltpu

F32 = jnp.float32
BF16 = jnp.bfloat16

GRID_W = 64
DA_HEADS = 4
DA_HEAD_DIM = 64
DA_V_DIM = 128
DA_WIDTH = 512
HG_HEADS = 4
HG_HEAD_DIM = 128
HG_WIDTH = 512
N_EXPERTS = 16
CAPACITY_FACTOR = 2
ROPE_THETA = 10000.0
NORM_EPS = 1e-6
LANES = 128
VMEM_LIMIT = 60 * 1024 * 1024
ATTN_TQ = 1024
ATTN_TK = 768
ATTN_ROW_BLOCK = 64
SCATTER_GROUP = 8


def _cparams(sem, vmem=VMEM_LIMIT):
    return pltpu.CompilerParams(dimension_semantics=sem, vmem_limit_bytes=vmem)


def _dot(a, b):
    return jnp.dot(a, b, preferred_element_type=F32)


def _dot_nt(a, b):
    return lax.dot_general(a, b, (((1,), (1,)), ((), ())), preferred_element_type=F32)


def _dot_tn(a, b):
    return lax.dot_general(a, b, (((0,), (0,)), ((), ())), preferred_element_type=F32)


def _split3(x):
    hi = x.astype(BF16)
    r1 = x - hi.astype(F32)
    mid = r1.astype(BF16)
    lo = (r1 - mid.astype(F32)).astype(BF16)
    return hi, mid, lo


def _mod_kernel(s_ref, w_ref, b_ref, o_ref):
    s = s_ref[...]
    s = s * jax.nn.sigmoid(s)
    hi, mid, lo = _split3(s)
    w = w_ref[...]
    whi, wmid, wlo = _split3(w)
    acc = _dot(hi, whi) + (_dot(hi, wmid) + _dot(mid, whi)) + (_dot(hi, wlo) + _dot(mid, wmid) + _dot(lo, whi))
    o_ref[...] = acc + b_ref[...]


def _mod_call(cmat, w_ada, b_ada):
    d = cmat.shape[1]
    n = w_ada.shape[1]
    tn = 1536 if n % 1536 == 0 else n
    return pl.pallas_call(
        _mod_kernel,
        grid=(n // tn,),
        in_specs=[pl.BlockSpec((8, d), lambda j: (0, 0)),
                  pl.BlockSpec((d, tn), lambda j: (0, j)),
                  pl.BlockSpec((1, tn), lambda j: (0, j))],
        out_specs=pl.BlockSpec((8, tn), lambda j: (0, j)),
        out_shape=jax.ShapeDtypeStruct((8, n), F32),
        compiler_params=_cparams(("arbitrary",)),
        name="mod",
    )(cmat, w_ada, b_ada)


def _inproj_kernel(x_ref, ctx_ref, mod_ref, g_ref, w_ref, cos_ref, sin_ref, qkv_ref, z_ref, iqg_ref, *, n_lat, d):
    b = pl.program_id(0)
    i = pl.program_id(1)
    is_ctx = i == n_lat
    xb = jnp.where(is_ctx, ctx_ref[0], x_ref[0])
    row = jnp.where(is_ctx, 2, b)
    sh = mod_ref[pl.ds(row, 1), 0:d]
    sc = mod_ref[pl.ds(row, 1), d:2 * d]
    ms = jnp.mean(xb * xb, axis=-1, keepdims=True)
    h = xb * lax.rsqrt(ms + NORM_EPS) * g_ref[...]
    h = h * (1.0 + sc) + sh
    p = _dot(h.astype(BF16), w_ref[...])
    cos = jnp.tile(cos_ref[...], (1, 8))
    sin = jnp.tile(sin_ref[...], (1, 8))
    qk = p[:, 0:1024] * cos + p[:, 4096:5120] * sin
    qkv_ref[0, :, 0:1024] = qk.astype(BF16)
    qkv_ref[0, :, 1024:1536] = p[:, 1024:1536].astype(BF16)
    z_ref[0] = p[:, 1536:2560]
    iqg_ref[0] = p[:, 2560:4096].astype(BF16)


def _inproj_call(x, ctx, mod, g, w, cos, sin):
    bsz, t, d = x.shape
    tm = ctx.shape[1]
    n_lat = t // tm
    rows = t + tm
    kern = functools.partial(_inproj_kernel, n_lat=n_lat, d=d)
    return pl.pallas_call(
        kern,
        grid=(bsz, n_lat + 1),
        in_specs=[pl.BlockSpec((1, tm, d), lambda b, i: (b, jnp.minimum(i, n_lat - 1), 0)),
                  pl.BlockSpec((1, tm, d), lambda b, i: (b, 0, 0)),
                  pl.BlockSpec(mod.shape, lambda b, i: (0, 0)),
                  pl.BlockSpec((1, d), lambda b, i: (0, 0)),
                  pl.BlockSpec(w.shape, lambda b, i: (0, 0)),
                  pl.BlockSpec((tm, LANES), lambda b, i: (i, 0)),
                  pl.BlockSpec((tm, LANES), lambda b, i: (i, 0))],
        out_specs=[pl.BlockSpec((1, tm, 1536), lambda b, i: (b, i, 0)),
                   pl.BlockSpec((1, tm, 1024), lambda b, i: (b, i, 0)),
                   pl.BlockSpec((1, tm, 1536), lambda b, i: (b, i, 0))],
        out_shape=[jax.ShapeDtypeStruct((bsz, rows, 1536), BF16),
                   jax.ShapeDtypeStruct((bsz, rows, 1024), F32),
                   jax.ShapeDtypeStruct((bsz, rows, 1536), BF16)],
        compiler_params=_cparams(("parallel", "arbitrary")),
        name="inproj",
    )(x, ctx, mod, g, w, cos, sin)


def _attn_kernel(lam_ref, q_ref, k_ref, v_ref, subln_ref, o_ref, qs_ref, sa_ref, sb_ref, p_ref, m_ref, l_ref,
                 acc_ref, *, tq, tk, nchunk, out_scale):
    q = q_ref[0]
    lane = lax.broadcasted_iota(jnp.int32, q.shape, 1)
    zero = jnp.zeros_like(q)
    qs_ref[0:tq, :] = jnp.where(lane < DA_HEAD_DIM, q, zero)
    qs_ref[tq:2 * tq, :] = jnp.where(lane >= DA_HEAD_DIM, q, zero)

    m_ref[...] = jnp.full(m_ref.shape, -1e30, F32)
    l_ref[...] = jnp.zeros(l_ref.shape, F32)
    acc_ref[...] = jnp.zeros(acc_ref.shape, F32)

    def scores(j, s_ref):
        kj = k_ref[0, pl.ds(pl.multiple_of(j * tk, tk), tk), :]
        s_ref[...] = _dot_nt(qs_ref[...], kj)

    def update(j, s_ref):
        vj = v_ref[0, pl.ds(pl.multiple_of(j * tk, tk), tk), :]
        m = m_ref[...]
        m_new = jnp.maximum(m, jnp.broadcast_to(jnp.max(s_ref[...], axis=-1, keepdims=True), m.shape))
        alpha = jnp.exp2(m - m_new)
        m_ref[...] = m_new
        rb = min(ATTN_ROW_BLOCK, 2 * tq)
        for r0 in range(0, 2 * tq, rb):
            rows = slice(r0, r0 + rb)
            mb = m_ref[rows, :]
            lsum = None
            for c0 in range(0, tk, LANES):
                p = jnp.exp2(s_ref[rows, c0:c0 + LANES] - mb)
                p_ref[rows, c0:c0 + LANES] = p.astype(BF16)
                lsum = p if lsum is None else lsum + p
            l_ref[rows, :] = alpha[rows] * l_ref[rows, :] + lsum
        acc_ref[...] = alpha * acc_ref[...] + _dot(p_ref[...], vj)

    npairs = (nchunk - 1) // 2
    scores(0, sa_ref)

    def body(i, _):
        scores(2 * i + 1, sb_ref)
        update(2 * i, sa_ref)
        scores(2 * i + 2, sa_ref)
        update(2 * i + 1, sb_ref)
        return 0
    lax.fori_loop(0, npairs, body, 0)
    if nchunk - 1 == 2 * npairs:
        update(nchunk - 1, sa_ref)
    else:
        scores(nchunk - 1, sb_ref)
        update(nchunk - 2, sa_ref)
        update(nchunk - 1, sb_ref)

    r = 1.0 / jnp.sum(l_ref[...], axis=-1, keepdims=True)
    acc = acc_ref[...]
    o = acc[0:tq] * r[0:tq] - lam_ref[0, 0] * (acc[tq:2 * tq] * r[tq:2 * tq])
    ms = jnp.mean(o * o, axis=-1, keepdims=True)
    o = o * lax.rsqrt(ms + NORM_EPS) * subln_ref[...] * out_scale
    o_ref[0] = o.astype(o_ref.dtype)


def _attn_call(lam, qkv, subln, t, tq, tk, out_scale):
    bsz, rows, _ = qkv.shape
    kern = functools.partial(_attn_kernel, tq=tq, tk=tk, nchunk=rows // tk, out_scale=out_scale)
    return pl.pallas_call(
        kern,
        grid=(bsz, DA_HEADS, t // tq),
        in_specs=[pl.BlockSpec(memory_space=pltpu.SMEM),
                  pl.BlockSpec((1, tq, LANES), lambda b, h, i: (b, i, h)),
                  pl.BlockSpec((1, rows, LANES), lambda b, h, i: (b, 0, DA_HEADS + h)),
                  pl.BlockSpec((1, rows, LANES), lambda b, h, i: (b, 0, 2 * DA_HEADS + h)),
                  pl.BlockSpec((1, LANES), lambda b, h, i: (0, 0))],
        out_specs=pl.BlockSpec((1, tq, LANES), lambda b, h, i: (b, i, h)),
        out_shape=jax.ShapeDtypeStruct((bsz, t, DA_WIDTH), BF16),
        scratch_shapes=[pltpu.VMEM((2 * tq, LANES), BF16),
                        pltpu.VMEM((2 * tq, tk), F32),
                        pltpu.VMEM((2 * tq, tk), F32),
                        pltpu.VMEM((2 * tq, tk), BF16),
                        pltpu.VMEM((2 * tq, LANES), F32),
                        pltpu.VMEM((2 * tq, LANES), F32),
                        pltpu.VMEM((2 * tq, LANES), F32)],
        compiler_params=_cparams(("parallel", "parallel", "arbitrary")),
        name="attn",
    )(lam, qkv, qkv, qkv, subln)


def _hgrn_consts(c):
    nl = int(math.log2(c))
    t = np.arange(c)
    lm = np.zeros((2, (nl + 2) * c, c), np.float32)
    for d in range(2):
        pi = t if d == 0 else c - 1 - t
        pr, pu = pi[:, None], pi[None, :]
        for lg in range(nl):
            same = (pr >> lg) == (pu >> lg)
            sec = ((pr >> lg) & 1) == 1
            m = same & np.where(sec, pu <= pr, pu > pr)
            lm[d, lg * c:(lg + 1) * c] = m
        lm[d, nl * c:(nl + 1) * c] = pu <= pr
        lm[d, (nl + 1) * c:(nl + 2) * c] = pu > pr
    masks = np.zeros((nl + 1, c, c), np.float32)
    for lg in range(nl):
        masks[lg] = (t[:, None] >> (lg + 1)) == (t[None, :] >> (lg + 1))
    masks[nl] = np.eye(c)
    return jnp.asarray(lm, BF16), jnp.asarray(masks, F32), nl


def _hgrn_kernel(z_ref, i_ref, q_ref, lb_ref, lmat_ref, mask_ref, o_ref, st_ref, *, c, nl):
    d = pl.program_id(2)
    s = pl.program_id(3)

    @pl.when(s == 0)
    def _():
        st_ref[...] = jnp.zeros(st_ref.shape, F32)

    z = z_ref[0]
    lb = lb_ref[0, 0]
    f = lb + (1.0 - lb) * jax.nn.sigmoid(z)
    lf = jnp.log(f)
    kk = 1.0 - f
    v = i_ref[0]
    q = q_ref[0].astype(F32)
    hi = lf.astype(BF16)
    lo = (lf - hi.astype(F32)).astype(BF16)
    x2 = _dot(lmat_ref[0], jnp.concatenate([hi, lo], axis=-1))
    x = x2[:, 0:LANES] + x2[:, LANES:2 * LANES]

    row = lax.broadcasted_iota(jnp.int32, (c, LANES), 0)
    pi = jnp.where(d == 0, row, c - 1 - row)
    scores = mask_ref[nl] * _dot_nt(q.astype(BF16), kk.astype(BF16))
    for lg in range(nl):
        g = jnp.exp(x[lg * c:(lg + 1) * c])
        sec = ((pi >> lg) & 1) == 1
        qt = jnp.where(sec, q * g, 0.0).astype(BF16)
        kt = jnp.where(sec, 0.0, kk * g).astype(BF16)
        scores = scores + mask_ref[lg] * _dot_nt(qt, kt)
    bpre = x[nl * c:(nl + 1) * c]
    bsuf = x[(nl + 1) * c:(nl + 2) * c]
    bend = jnp.where(d == 0, bpre[c - 1:c], bpre[0:1])
    st = st_ref[...]
    o = _dot_nt((q * jnp.exp(bpre)).astype(BF16), st.astype(BF16)) + _dot(scores.astype(BF16), v)
    kb = (kk * jnp.exp(bsuf)).astype(BF16)
    st_ref[...] = st * jnp.exp(bend) + _dot_tn(v, kb)
    o_ref[0, 0] = o.astype(o_ref.dtype)


def _hgrn_call(z, iqg, lb4, c):
    bsz, rows, _ = z.shape
    nb = rows // c - 1
    lmat, masks, nl = _hgrn_consts(c)

    def blk(d, s):
        return jnp.where(s == 0, nb, jnp.where(d == 0, s - 1, nb - s))

    kern = functools.partial(_hgrn_kernel, c=c, nl=nl)
    return pl.pallas_call(
        kern,
        grid=(bsz, HG_HEADS, 2, nb + 1),
        in_specs=[pl.BlockSpec((1, c, LANES), lambda b, h, d, s: (b, blk(d, s), d * HG_HEADS + h)),
                  pl.BlockSpec((1, c, LANES), lambda b, h, d, s: (b, blk(d, s), h)),
                  pl.BlockSpec((1, c, LANES), lambda b, h, d, s: (b, blk(d, s), HG_HEADS + h)),
                  pl.BlockSpec((1, 1, 1, LANES), lambda b, h, d, s: (d, h, 0, 0)),
                  pl.BlockSpec((1,) + lmat.shape[1:], lambda b, h, d, s: (d, 0, 0)),
                  pl.BlockSpec(masks.shape, lambda b, h, d, s: (0, 0, 0))],
        out_specs=pl.BlockSpec((1, 1, c, LANES), lambda b, h, d, s: (b, d, blk(d, s), h)),
        out_shape=jax.ShapeDtypeStruct((bsz, 2, rows, HG_WIDTH), BF16),
        scratch_shapes=[pltpu.VMEM((HG_HEAD_DIM, HG_HEAD_DIM), F32)],
        compiler_params=_cparams(("parallel", "parallel", "parallel", "arbitrary")),
        name="hgrn",
    )(z, iqg, iqg, lb4, lmat, masks)


def _post_kernel(da_ref, of_ref, ob_ref, g_ref, x_ref, mod_ref, hgn_ref, wout_ref, npost_ref, npre_ref,
                 wrh_ref, wrl_ref, x1_ref, h2_ref, aff_ref, *, d):
    b = pl.program_id(0)
    o = of_ref[0, 0].astype(F32) + ob_ref[0, 0].astype(F32)
    parts = []
    for h in range(HG_HEADS):
        oh = o[:, h * LANES:(h + 1) * LANES]
        ms = jnp.mean(oh * oh, axis=-1, keepdims=True)
        parts.append(oh * lax.rsqrt(ms + NORM_EPS) * hgn_ref[...])
    g = g_ref[0].astype(F32)
    ohn = jnp.concatenate(parts, axis=-1) * (g * jax.nn.sigmoid(g))
    mixed = jnp.concatenate([da_ref[0], ohn.astype(BF16)], axis=-1)
    y = _dot(mixed, wout_ref[...])
    yn = y * lax.rsqrt(jnp.mean(y * y, axis=-1, keepdims=True) + NORM_EPS) * npost_ref[...]
    gt1 = mod_ref[pl.ds(b, 1), 2 * d:3 * d]
    sh2 = mod_ref[pl.ds(b, 1), 3 * d:4 * d]
    sc2 = mod_ref[pl.ds(b, 1), 4 * d:5 * d]
    x1 = x_ref[0] + gt1 * yn
    x1_ref[0] = x1
    h2 = x1 * lax.rsqrt(jnp.mean(x1 * x1, axis=-1, keepdims=True) + NORM_EPS) * npre_ref[...]
    h2 = h2 * (1.0 + sc2) + sh2
    h2_ref[0] = h2
    hh = h2.astype(BF16)
    hl = (h2 - hh.astype(F32)).astype(BF16)
    logits = _dot(hh, wrh_ref[...]) + (_dot(hl, wrh_ref[...]) + _dot(hh, wrl_ref[...]))
    mx = jnp.max(logits, axis=-1, keepdims=True)
    e = jnp.exp(logits - mx)
    aff_ref[0] = e / jnp.sum(e, axis=-1, keepdims=True)


def _post_call(out_da, o_hg, iqg, x, mod, hg_norm, w_out, npost, npre, wr_hi, wr_lo, tm):
    bsz, t, d = x.shape
    kern = functools.partial(_post_kernel, d=d)
    row = lambda b, i: (0, 0)
    return pl.pallas_call(
        kern,
        grid=(bsz, t // tm),
        in_specs=[pl.BlockSpec((1, tm, DA_WIDTH), lambda b, i: (b, i, 0)),
                  pl.BlockSpec((1, 1, tm, HG_WIDTH), lambda b, i: (b, 0, i, 0)),
                  pl.BlockSpec((1, 1, tm, HG_WIDTH), lambda b, i: (b, 1, i, 0)),
                  pl.BlockSpec((1, tm, HG_WIDTH), lambda b, i: (b, i, 2)),
                  pl.BlockSpec((1, tm, d), lambda b, i: (b, i, 0)),
                  pl.BlockSpec(mod.shape, row),
                  pl.BlockSpec((1, LANES), row),
                  pl.BlockSpec(w_out.shape, row),
                  pl.BlockSpec((1, d), row),
                  pl.BlockSpec((1, d), row),
                  pl.BlockSpec(wr_hi.shape, row),
                  pl.BlockSpec(wr_lo.shape, row)],
        out_specs=[pl.BlockSpec((1, tm, d), lambda b, i: (b, i, 0)),
                   pl.BlockSpec((1, tm, d), lambda b, i: (b, i, 0)),
                   pl.BlockSpec((1, tm, N_EXPERTS), lambda b, i: (b, i, 0))],
        out_shape=[jax.ShapeDtypeStruct((bsz, t, d), F32),
                   jax.ShapeDtypeStruct((bsz, t, d), F32),
                   jax.ShapeDtypeStruct((bsz, t, N_EXPERTS), F32)],
        compiler_params=_cparams(("parallel", "arbitrary")),
        name="post",
    )(out_da, o_hg, o_hg, iqg, x, mod, hg_norm, w_out, npost, npre, wr_hi, wr_lo)


def _topk_kernel(aff_ref, ltri_ref, idx_ref, gate_ref, selc_ref, cnt_ref, cnt_smem, acci_ref, accg_ref,
                 *, t, cap, rb):
    ne = N_EXPERTS
    nblk = t // rb

    def as_float(bits):
        return lax.bitcast_convert_type(bits, F32)

    def count_ge(thr):
        def body(i, acc):
            a = aff_ref[0, pl.ds(i * rb, rb), :]
            return acc + jnp.sum(jnp.where(a >= thr, 1, 0), axis=0, keepdims=True)
        return lax.fori_loop(0, nblk, body, jnp.zeros((1, ne), jnp.int32))

    def bit_body(k, tau):
        cand = tau | (jnp.int32(1) << (29 - k))
        return jnp.where(count_ge(as_float(cand)) >= cap, cand, tau)
    tau = lax.fori_loop(0, 30, bit_body, jnp.zeros((1, ne), jnp.int32))
    tau_f = as_float(tau)
    nxt_f = as_float(tau + 1)

    n_gt = count_ge(nxt_f)
    need = (cap - n_gt).astype(F32)

    ltri = ltri_ref[...]

    def sel_body(i, carry):
        ceq, csel = carry
        a = aff_ref[0, pl.ds(i * rb, rb), :]
        gt = a >= nxt_f
        eq = (a >= tau_f) & jnp.logical_not(gt)
        eq_rank = _dot(ltri, jnp.where(eq, 1.0, 0.0).astype(BF16)) + ceq
        sel = gt | (eq & (eq_rank <= need))
        sel_f = jnp.where(sel, 1.0, 0.0)
        c_incl = _dot(ltri, sel_f.astype(BF16)) + csel
        selc_ref[pl.ds(i * rb, rb), :] = jnp.where(sel, c_incl, 0.0)
        cnt_ref[pl.ds(i, 1), :] = csel.astype(jnp.int32)
        return eq_rank[rb - 1:rb], c_incl[rb - 1:rb]
    zero = jnp.zeros((1, ne), F32)
    _, c_all = lax.fori_loop(0, nblk, sel_body, (zero, zero))
    cnt_ref[pl.ds(nblk, 1), :] = c_all.astype(jnp.int32)

    def to_smem(i, _):
        row = cnt_ref[pl.ds(i, 1), :]
        for e in range(ne):
            cnt_smem[i * ne + e] = row[0, e]
        return 0
    lax.fori_loop(0, nblk + 1, to_smem, 0)

    n_pt = cap // LANES
    acci_ref[...] = jnp.zeros(acci_ref.shape, F32)
    accg_ref[...] = jnp.zeros(accg_ref.shape, F32)
    lane = lax.broadcasted_iota(jnp.int32, (1, LANES), 1)
    for e in range(ne):
        def slot_body(i, _):
            cc = jnp.broadcast_to(selc_ref[pl.ds(i * rb, rb), e:e + 1], (rb, LANES))
            aa = jnp.broadcast_to(aff_ref[0, pl.ds(i * rb, rb), e:e + 1], (rb, LANES))
            tt = (lax.broadcasted_iota(jnp.int32, (rb, LANES), 0) + i * rb).astype(F32)
            first = cnt_smem[i * ne + e]
            last = cnt_smem[(i + 1) * ne + e] - 1

            def tile_body(j, _):
                pv = (lane + (j * LANES + 1)).astype(F32)
                hit = cc == pv
                ti = jnp.sum(jnp.where(hit, tt, 0.0).reshape(rb // 8, 8, LANES), axis=0)
                gi = jnp.sum(jnp.where(hit, aa, 0.0).reshape(rb // 8, 8, LANES), axis=0)
                acci_ref[e * n_pt + j] = acci_ref[e * n_pt + j] + ti
                accg_ref[e * n_pt + j] = accg_ref[e * n_pt + j] + gi
                return 0
            lax.fori_loop(first // LANES, last // LANES + 1, tile_body, 0)
            return 0
        lax.fori_loop(0, nblk, slot_body, 0)
        for j in range(n_pt):
            idx_ref[0, e:e + 1, j * LANES:(j + 1) * LANES] = (
                jnp.sum(acci_ref[e * n_pt + j], axis=0, keepdims=True).astype(jnp.int32))
            gate_ref[0, e:e + 1, j * LANES:(j + 1) * LANES] = jnp.sum(accg_ref[e * n_pt + j], axis=0, keepdims=True)


def _topk_call(aff, cap):
    bsz, t, ne = aff.shape
    rb = 256
    ltri = jnp.asarray(np.tril(np.ones((rb, rb), np.float32)), BF16)
    kern = functools.partial(_topk_kernel, t=t, cap=cap, rb=rb)
    return pl.pallas_call(
        kern,
        grid=(bsz,),
        in_specs=[pl.BlockSpec((1, t, ne), lambda b: (b, 0, 0)),
                  pl.BlockSpec((rb, rb), lambda b: (0, 0))],
        out_specs=[pl.BlockSpec((1, ne, cap), lambda b: (b, 0, 0)),
                   pl.BlockSpec((1, ne, cap), lambda b: (b, 0, 0))],
        out_shape=[jax.ShapeDtypeStruct((bsz, ne, cap), jnp.int32),
                   jax.ShapeDtypeStruct((bsz, ne, cap), F32)],
        scratch_shapes=[pltpu.VMEM((t, ne), F32),
                        pltpu.VMEM((-(-(t // rb + 1) // 8) * 8, ne), jnp.int32),
                        pltpu.SMEM(((t // rb + 1) * ne,), jnp.int32),
                        pltpu.VMEM((ne * (cap // LANES), 8, LANES), F32),
                        pltpu.VMEM((ne * (cap // LANES), 8, LANES), F32)],
        compiler_params=_cparams(("arbitrary",)),
        name="topk",
    )(aff, ltri)


def _moe_kernel(idx_ref, h2_ref, gate_ref, wg_ref, wu_ref, wd_ref, f_ref,
                f_scr, xin, xbf, yacc, gsem, osem, *, t, cap, ne, nf, zr):
    b = pl.program_id(0)
    e = pl.program_id(1)
    fc = pl.program_id(2)
    base = (b * ne + e) * cap

    @pl.when((e == 0) & (fc == 0))
    def _():
        def zbody(i, _):
            f_scr[pl.ds(i * zr, zr), :] = jnp.zeros((zr, f_scr.shape[1]), F32)
            return 0
        lax.fori_loop(0, t // zr, zbody, 0)

    def row_copy(p):
        tok = idx_ref[base + p]
        return pltpu.make_async_copy(h2_ref.at[b, pl.ds(tok, 1), :], xin.at[pl.ds(p, 1), :], gsem)

    @pl.when(fc == 0)
    def _():
        def gstart(p, _):
            row_copy(p).start()
            return 0
        lax.fori_loop(0, cap, gstart, 0, unroll=8)
        pltpu.make_async_copy(h2_ref.at[b, pl.ds(0, cap), :], xin, gsem).wait()
        xbf[...] = xin[...].astype(BF16)

    x = xbf[...]
    g = _dot(x, wg_ref[0].astype(BF16))
    u = _dot(x, wu_ref[0].astype(BF16))
    hmid = (g * jax.nn.sigmoid(g) * u).astype(BF16)
    contrib = _dot(hmid, wd_ref[0].astype(BF16))

    @pl.when(fc == 0)
    def _():
        yacc[...] = contrib

    @pl.when(fc != 0)
    def _():
        yacc[...] = yacc[...] + contrib

    @pl.when(fc == nf - 1)
    def _():
        yacc[...] = yacc[...] * gate_ref[0, 0]

        def sbody(g, _):
            p0 = g * SCATTER_GROUP
            toks = [idx_ref[base + p0 + i] for i in range(SCATTER_GROUP)]
            yr = [yacc[pl.ds(p0 + i, 1), :] for i in range(SCATTER_GROUP)]
            fr = [f_scr[pl.ds(toks[i], 1), :] for i in range(SCATTER_GROUP)]
            for i in range(SCATTER_GROUP):
                f_scr[pl.ds(toks[i], 1), :] = fr[i] + yr[i]
            return 0
        lax.fori_loop(0, cap // SCATTER_GROUP, sbody, 0)

    @pl.when((e == ne - 1) & (fc == nf - 1))
    def _():
        cp = pltpu.make_async_copy(f_scr, f_ref.at[b], osem)
        cp.start()
        cp.wait()


def _moe_call(idx_flat, h2, gate4, w_gate, w_up, w_down, cap, tf):
    bsz, t, d = h2.shape
    ne, _, ff = w_gate.shape
    nf = ff // tf
    kern = functools.partial(_moe_kernel, t=t, cap=cap, ne=ne, nf=nf, zr=256)
    grid_spec = pltpu.PrefetchScalarGridSpec(
        num_scalar_prefetch=1,
        grid=(bsz, ne, nf),
        in_specs=[pl.BlockSpec(memory_space=pl.ANY),
                  pl.BlockSpec((1, 1, cap, 1), lambda b, e, f, idx: (b, e, 0, 0)),
                  pl.BlockSpec((1, d, tf), lambda b, e, f, idx: (e, 0, f)),
                  pl.BlockSpec((1, d, tf), lambda b, e, f, idx: (e, 0, f)),
                  pl.BlockSpec((1, tf, d), lambda b, e, f, idx: (e, f, 0))],
        out_specs=pl.BlockSpec(memory_space=pl.ANY),
        scratch_shapes=[pltpu.VMEM((t, d), F32),
                        pltpu.VMEM((cap, d), F32),
                        pltpu.VMEM((cap, d), BF16),
                        pltpu.VMEM((cap, d), F32),
                        pltpu.SemaphoreType.DMA(()),
                        pltpu.SemaphoreType.DMA(())],
    )
    return pl.pallas_call(
        kern,
        grid_spec=grid_spec,
        out_shape=jax.ShapeDtypeStruct((bsz, t, d), F32),
        compiler_params=_cparams(("arbitrary", "arbitrary", "arbitrary")),
        name="moe",
    )(idx_flat, h2, gate4, w_gate, w_up, w_down)


def _final_kernel(x1_ref, f_ref, mod_ref, n_ref, o_ref, *, d):
    b = pl.program_id(0)
    f = f_ref[0]
    fn = f * lax.rsqrt(jnp.mean(f * f, axis=-1, keepdims=True) + NORM_EPS) * n_ref[...]
    gt2 = mod_ref[pl.ds(b, 1), 5 * d:6 * d]
    o_ref[0] = x1_ref[0] + gt2 * fn


def _final_call(x1, f, mod, npost, tm):
    bsz, t, d = x1.shape
    kern = functools.partial(_final_kernel, d=d)
    return pl.pallas_call(
        kern,
        grid=(bsz, t // tm),
        in_specs=[pl.BlockSpec((1, tm, d), lambda b, i: (b, i, 0)),
                  pl.BlockSpec((1, tm, d), lambda b, i: (b, i, 0)),
                  pl.BlockSpec(mod.shape, lambda b, i: (0, 0)),
                  pl.BlockSpec((1, d), lambda b, i: (0, 0))],
        out_specs=pl.BlockSpec((1, tm, d), lambda b, i: (b, i, 0)),
        out_shape=jax.ShapeDtypeStruct((bsz, t, d), F32),
        compiler_params=_cparams(("parallel", "arbitrary")),
        name="final",
    )(x1, f, mod, npost)


def _rope_tables(t, ctx_len):
    half = DA_HEAD_DIM // 2
    inv_freq = 1.0 / (ROPE_THETA ** (jnp.arange(0, half, 2, dtype=F32) / half))
    pos = jnp.arange(t, dtype=jnp.int32)
    r = (pos // GRID_W).astype(F32)
    col = (pos % GRID_W).astype(F32)
    ang_r = r[:, None] * inv_freq
    ang_c = col[:, None] * inv_freq
    ang = jnp.concatenate([ang_r, ang_r, ang_c, ang_c], axis=-1)
    cos = jnp.concatenate([jnp.cos(ang), jnp.ones((ctx_len, DA_HEAD_DIM), F32)], axis=0)
    sin = jnp.concatenate([jnp.sin(ang), jnp.zeros((ctx_len, DA_HEAD_DIM), F32)], axis=0)
    return jnp.tile(cos, (1, 2)), jnp.tile(sin, (1, 2))


def _rot_columns(w):
    k, n = w.shape
    w4 = w.reshape(k, n // DA_HEAD_DIM, 4, DA_HEAD_DIM // 4)
    return jnp.stack([-w4[:, :, 1], w4[:, :, 0], -w4[:, :, 3], w4[:, :, 2]], axis=2).reshape(k, n)


def kernel(x, c, ctx, c_ctx, w_ada, b_ada, norm_pre_mix, norm_post_mix, norm_pre_ffn, norm_post_ffn, w_in,
           da_lambda_q1, da_lambda_k1, da_lambda_q2, da_lambda_k2, da_subln, hg_lower_bound, hg_norm, w_out,
           w_router, w_gate, w_up, w_down):
    bsz, t, d = x.shape
    ctx_len = ctx.shape[1]
    assert bsz <= 2 and d == 1024 and w_ada.shape[0] == 1
    cap = CAPACITY_FACTOR * t // N_EXPERTS
    lambda_init = 0.8 - 0.6 * math.exp(-0.3 * 0)

    lam = (jnp.exp(jnp.sum(da_lambda_q1[0].astype(F32) * da_lambda_k1[0].astype(F32)))
           - jnp.exp(jnp.sum(da_lambda_q2[0].astype(F32) * da_lambda_k2[0].astype(F32))) + lambda_init)
    lam = lam.reshape(1, 1)
    lb = jnp.cumsum(jax.nn.softmax(hg_lower_bound.astype(F32), axis=0), axis=0)[0]
    lb4 = lb.reshape(2, HG_HEADS, 1, HG_HEAD_DIM)

    wi = w_in[0]
    scale = DA_HEAD_DIM ** -0.5 * math.log2(math.e)
    wq = wi[:, 0:DA_WIDTH] * scale
    wk = wi[:, DA_WIDTH:2 * DA_WIDTH]
    w_all = jnp.concatenate([wq, wk, wi[:, 2 * DA_WIDTH:], _rot_columns(wq), _rot_columns(wk)],
                            axis=1).astype(BF16)
    cos, sin = _rope_tables(t, ctx_len)

    cmat = jnp.concatenate([c, c_ctx[None, :], jnp.zeros((8 - bsz - 1, d), F32)], axis=0)
    mod = _mod_call(cmat, w_ada[0], b_ada[0][None, :])

    qkv, z, iqg = _inproj_call(x, ctx, mod, norm_pre_mix[0][None, :], w_all, cos, sin)

    tq = min(ATTN_TQ, t)
    nk = t + ctx_len
    tk = ATTN_TK if nk % ATTN_TK == 0 else ctx_len
    out_da = _attn_call(lam, qkv, da_subln[0][None, :], t, tq, tk, 1.0 - lambda_init)
    o_hg = _hgrn_call(z, iqg, lb4, ctx_len)

    wr = w_router[0]
    wr_hi = wr.astype(BF16)
    wr_lo = (wr - wr_hi.astype(F32)).astype(BF16)
    x1, h2, aff = _post_call(out_da, o_hg, iqg, x, mod, hg_norm[0][None, :], w_out[0].astype(BF16),
                             norm_post_mix[0][None, :], norm_pre_ffn[0][None, :], wr_hi, wr_lo, 256)
    idx, gate = _topk_call(aff, cap)
    f = _moe_call(idx.reshape(-1), h2, gate.reshape(bsz, N_EXPERTS, cap, 1), w_gate[0], w_up[0], w_down[0],
                  cap, 256)
    return _final_call(x1, f, mod, norm_post_ffn[0][None, :], 256)
```

```python
import functools
import math

import jax
import jax.numpy as jnp
import numpy as np
from jax import lax
from jax.experimental import pallas as pl
from jax.experimental.pallas import tpu as pltpu

F32 = jnp.float32
BF16 = jnp.bfloat16

GRID_W = 64
DA_HEADS = 4
DA_HEAD_DIM = 64
DA_V_DIM = 128
DA_WIDTH = 512
HG_HEADS = 4
HG_HEAD_DIM = 128
HG_WIDTH = 512
N_EXPERTS = 16
CAPACITY_FACTOR = 2
ROPE_THETA = 10000.0
NORM_EPS = 1e-6
LANES = 128
VMEM_LIMIT = 60 * 1024 * 1024
ATTN_TQ = 1024
ATTN_TK = 768
ATTN_ROW_BLOCK = 64
SCATTER_GROUP = 8


def _cparams(sem, vmem=VMEM_LIMIT):
    return pltpu.CompilerParams(dimension_semantics=sem, vmem_limit_bytes=vmem)


def _dot(a, b):
    return jnp.dot(a, b, preferred_element_type=F32)


def _dot_nt(a, b):
    return lax.dot_general(a, b, (((1,), (1,)), ((), ())), preferred_element_type=F32)


def _dot_tn(a, b):
    return lax.dot_general(a, b, (((0,), (0,)), ((), ())), preferred_element_type=F32)


def _split3(x):
    hi = x.astype(BF16)
    r1 = x - hi.astype(F32)
    mid = r1.astype(BF16)
    lo = (r1 - mid.astype(F32)).astype(BF16)
    return hi, mid, lo


def _mod_kernel(s_ref, w_ref, b_ref, o_ref):
    s = s_ref[...]
    s = s * jax.nn.sigmoid(s)
    hi, mid, lo = _split3(s)
    w = w_ref[...]
    whi, wmid, wlo = _split3(w)
    acc = _dot(hi, whi) + (_dot(hi, wmid) + _dot(mid, whi)) + (_dot(hi, wlo) + _dot(mid, wmid) + _dot(lo, whi))
    o_ref[...] = acc + b_ref[...]


def _mod_call(cmat, w_ada, b_ada):
    d = cmat.shape[1]
    n = w_ada.shape[1]
    tn = 1536 if n % 1536 == 0 else n
    return pl.pallas_call(
        _mod_kernel,
        grid=(n // tn,),
        in_specs=[pl.BlockSpec((8, d), lambda j: (0, 0)),
                  pl.BlockSpec((d, tn), lambda j: (0, j)),
                  pl.BlockSpec((1, tn), lambda j: (0, j))],
        out_specs=pl.BlockSpec((8, tn), lambda j: (0, j)),
        out_shape=jax.ShapeDtypeStruct((8, n), F32),
        compiler_params=_cparams(("arbitrary",)),
        name="mod",
    )(cmat, w_ada, b_ada)


def _inproj_kernel(x_ref, ctx_ref, mod_ref, g_ref, w_ref, cos_ref, sin_ref, qkv_ref, z_ref, iqg_ref, *, n_lat, d):
    b = pl.program_id(0)
    i = pl.program_id(1)
    is_ctx = i == n_lat
    xb = jnp.where(is_ctx, ctx_ref[0], x_ref[0])
    row = jnp.where(is_ctx, 2, b)
    sh = mod_ref[pl.ds(row, 1), 0:d]
    sc = mod_ref[pl.ds(row, 1), d:2 * d]
    ms = jnp.mean(xb * xb, axis=-1, keepdims=True)
    h = xb * lax.rsqrt(ms + NORM_EPS) * g_ref[...]
    h = h * (1.0 + sc) + sh
    p = _dot(h.astype(BF16), w_ref[...])
    cos = jnp.tile(cos_ref[...], (1, 8))
    sin = jnp.tile(sin_ref[...], (1, 8))
    qk = p[:, 0:1024] * cos + p[:, 4096:5120] * sin
    qkv_ref[0, :, 0:1024] = qk.astype(BF16)
    qkv_ref[0, :, 1024:1536] = p[:, 1024:1536].astype(BF16)
    z_ref[0] = p[:, 1536:2560]
    iqg_ref[0] = p[:, 2560:4096].astype(BF16)


def _inproj_call(x, ctx, mod, g, w, cos, sin):
    bsz, t, d = x.shape
    tm = ctx.shape[1]
    n_lat = t // tm
    rows = t + tm
    kern = functools.partial(_inproj_kernel, n_lat=n_lat, d=d)
    return pl.pallas_call(
        kern,
        grid=(bsz, n_lat + 1),
        in_specs=[pl.BlockSpec((1, tm, d), lambda b, i: (b, jnp.minimum(i, n_lat - 1), 0)),
                  pl.BlockSpec((1, tm, d), lambda b, i: (b, 0, 0)),
                  pl.BlockSpec(mod.shape, lambda b, i: (0, 0)),
                  pl.BlockSpec((1, d), lambda b, i: (0, 0)),
                  pl.BlockSpec(w.shape, lambda b, i: (0, 0)),
                  pl.BlockSpec((tm, LANES), lambda b, i: (i, 0)),
                  pl.BlockSpec((tm, LANES), lambda b, i: (i, 0))],
        out_specs=[pl.BlockSpec((1, tm, 1536), lambda b, i: (b, i, 0)),
                   pl.BlockSpec((1, tm, 1024), lambda b, i: (b, i, 0)),
                   pl.BlockSpec((1, tm, 1536), lambda b, i: (b, i, 0))],
        out_shape=[jax.ShapeDtypeStruct((bsz, rows, 1536), BF16),
                   jax.ShapeDtypeStruct((bsz, rows, 1024), F32),
                   jax.ShapeDtypeStruct((bsz, rows, 1536), BF16)],
        compiler_params=_cparams(("parallel", "arbitrary")),
        name="inproj",
    )(x, ctx, mod, g, w, cos, sin)


def _attn_kernel(lam_ref, q_ref, k_ref, v_ref, subln_ref, o_ref, qs_ref, sa_ref, sb_ref, pa_ref, pb_ref, m_ref,
                 l_ref, acc_ref, *, tq, tk, nchunk, out_scale):
    q = q_ref[0]
    lane = lax.broadcasted_iota(jnp.int32, q.shape, 1)
    zero = jnp.zeros_like(q)
    qs_ref[0:tq, :] = jnp.where(lane < DA_HEAD_DIM, q, zero)
    qs_ref[tq:2 * tq, :] = jnp.where(lane >= DA_HEAD_DIM, q, zero)

    m_ref[...] = jnp.full(m_ref.shape, -1e30, F32)
    l_ref[...] = jnp.zeros(l_ref.shape, F32)
    acc_ref[...] = jnp.zeros(acc_ref.shape, F32)

    def scores(j, s_ref):
        kj = k_ref[0, pl.ds(pl.multiple_of(j * tk, tk), tk), :]
        s_ref[...] = _dot_nt(qs_ref[...], kj)

    def update(j, s_ref, p_ref):
        vj =v_ref[0, pl.ds(pl.multiple_of(j * tk, tk), tk), :]
        m = m_ref[...]
        m_new = jnp.maximum(m, jnp.broadcast_to(jnp.max(s_ref[...], axis=-1, keepdims=True), m.shape))
        alpha = jnp.exp2(m - m_new)
        m_ref[...] = m_new
        rb = min(ATTN_ROW_BLOCK, 2 * tq)
        for r0 in range(0, 2 * tq, rb):
            rows = slice(r0, r0 + rb)
            mb = m_ref[rows, :]
            lsum = None
            for c0 in range(0, tk, LANES):
                p = jnp.exp2(s_ref[rows, c0:c0 + LANES] - mb)
                p_ref[rows, c0:c0 + LANES] = p.astype(BF16)
                lsum = p if lsum is None else lsum + p
            l_ref[rows, :] = alpha[rows] * l_ref[rows, :] + lsum
        acc_ref[...] = alpha * acc_ref[...] + _dot(p_ref[...], vj)

    npairs = (nchunk - 1) // 2
    scores(0, sa_ref)

    def body(i, _):
        scores(2 * i + 1, sb_ref)
        update(2 * i, sa_ref, pa_ref)
        scores(2 * i + 2, sa_ref)
        update(2 * i + 1, sb_ref, pb_ref)
        return 0
    lax.fori_loop(0, npairs, body, 0)
    if nchunk - 1 == 2 * npairs:
        update(nchunk - 1, sa_ref, pa_ref)
    else:
        scores(nchunk - 1, sb_ref)
        update(nchunk - 2, sa_ref, pa_ref)
        update(nchunk - 1, sb_ref, pb_ref)

    r = 1.0 / jnp.sum(l_ref[...], axis=-1, keepdims=True)
    acc = acc_ref[...]
    o = acc[0:tq] * r[0:tq] - lam_ref[0, 0] * (acc[tq:2 * tq] * r[tq:2 * tq])
    ms = jnp.mean(o * o, axis=-1, keepdims=True)
    o = o * lax.rsqrt(ms + NORM_EPS) * subln_ref[...] * out_scale
    o_ref[0] = o.astype(o_ref.dtype)


def _attn_call(lam, qkv, subln, t, tq, tk, out_scale):
    bsz, rows, _ = qkv.shape
    kern = functools.partial(_attn_kernel, tq=tq, tk=tk, nchunk=rows // tk, out_scale=out_scale)
    return pl.pallas_call(
        kern,
        grid=(bsz, DA_HEADS, t // tq),
        in_specs=[pl.BlockSpec(memory_space=pltpu.SMEM),
                  pl.BlockSpec((1, tq, LANES), lambda b, h, i: (b, i, h)),
                  pl.BlockSpec((1, rows, LANES), lambda b, h, i: (b, 0, DA_HEADS + h)),
                  pl.BlockSpec((1, rows, LANES), lambda b, h, i: (b, 0, 2 * DA_HEADS + h)),
                  pl.BlockSpec((1, LANES), lambda b, h, i: (0, 0))],
        out_specs=pl.BlockSpec((1, tq, LANES), lambda b, h, i: (b, i, h)),
        out_shape=jax.ShapeDtypeStruct((bsz, t, DA_WIDTH), BF16),
        scratch_shapes=[pltpu.VMEM((2 * tq, LANES), BF16),
                        pltpu.VMEM((2 * tq, tk), F32),
                        pltpu.VMEM((2 * tq, tk), F32),
                        pltpu.VMEM((2 * tq, tk), BF16),
                        pltpu.VMEM((2 * tq, tk), BF16),
                        pltpu.VMEM((2 * tq, LANES), F32),
                        pltpu.VMEM((2 * tq, LANES), F32),
                        pltpu.VMEM((2 * tq, LANES), F32)],
        compiler_params=_cparams(("parallel", "parallel", "arbitrary")),
        name="attn",
    )(lam, qkv, qkv, qkv, subln)


HGRN_SUB_LEVELS = 3


def _hgrn_consts(c, d):
    nl = int(math.log2(c))
    t = np.arange(c)
    pi = t if d == 0 else c - 1 - t
    pr, pu = pi[:, None], pi[None, :]
    lm = np.zeros(((HGRN_SUB_LEVELS + 1) * c, c), np.float32)
    for lg in range(HGRN_SUB_LEVELS):
        same = (pr >> lg) == (pu >> lg)
        sec = ((pr >> lg) & 1) == 1
        lm[lg * c:(lg + 1) * c] = same & np.where(sec, pu <= pr, pu > pr)
    lm[HGRN_SUB_LEVELS * c:] = pu <= pr
    masks = np.zeros((nl + 1, c, c), np.float32)
    for lg in range(nl):
        masks[lg] = (t[:, None] >> (lg + 1)) == (t[None, :] >> (lg + 1))
    masks[nl] = np.eye(c)
    return jnp.asarray(lm, BF16), jnp.asarray(masks, F32), nl


def _hgrn_kernel(z_ref, i_ref, q_ref, lb_ref, lmat_ref, mask_ref, o_ref, st_ref, *, c, nl, d):
    s = pl.program_id(2)

    @pl.when(s == 0)
    def _():
        st_ref[...] = jnp.zeros(st_ref.shape, F32)

    z = z_ref[0]
    lb = lb_ref[0, 0]
    f = lb + (1.0 - lb) * jax.nn.sigmoid(z)
    lf = jnp.log(f)
    kk = 1.0 - f
    v = i_ref[0]
    q = q_ref[0].astype(F32)
    hi = lf.astype(BF16)
    lo = (lf - hi.astype(F32)).astype(BF16)
    x2 = _dot(lmat_ref[...], jnp.concatenate([hi, lo], axis=-1))
    x = x2[:, 0:LANES] + x2[:, LANES:2 * LANES]
    cum = x[HGRN_SUB_LEVELS * c:]
    last = c - 1 if d == 0 else 0
    bend = cum[last:last + 1]

    row = lax.broadcasted_iota(jnp.int32, (c, LANES), 0)
    scores = mask_ref[nl] * _dot_nt(q.astype(BF16), kk.astype(BF16))
    for lg in range(nl):
        hs = 1 << lg
        if lg < HGRN_SUB_LEVELS:
            xl = x[lg * c:(lg + 1) * c]
        else:
            mid = hs - 1 if d == 0 else hs
            c3 = cum.reshape(c // (2 * hs), 2 * hs, LANES)
            ref = jnp.broadcast_to(c3[:, mid:mid + 1, :], c3.shape).reshape(c, LANES)
            xl = -jnp.abs(cum - ref)
        g = jnp.exp(xl)
        sec = ((row >> lg) & 1) == (1 - d)
        qt = jnp.where(sec, q * g, 0.0).astype(BF16)
        kt = jnp.where(sec, 0.0, kk * g).astype(BF16)
        scores = scores + mask_ref[lg] * _dot_nt(qt, kt)
    st = st_ref[...]
    o = _dot_nt((q * jnp.exp(cum)).astype(BF16), st.astype(BF16)) + _dot(scores.astype(BF16), v)
    kb = (kk * jnp.exp(bend - cum)).astype(BF16)
    st_ref[...] = st * jnp.exp(bend) + _dot_tn(v, kb)
    o_ref[0] = o.astype(o_ref.dtype)


def _hgrn_call(z, iqg, lb4, c, d):
    bsz, rows, _ = z.shape
    nb = rows // c - 1
    lmat, masks, nl = _hgrn_consts(c, d)

    def blk(s):
        return jnp.where(s == 0, nb, s - 1 if d == 0 else nb - s)

    kern = functools.partial(_hgrn_kernel, c=c, nl=nl, d=d)
    return pl.pallas_call(
        kern,
        grid=(bsz, HG_HEADS, nb + 1),
        in_specs=[pl.BlockSpec((1, c, LANES), lambda b, h, s: (b, blk(s), d * HG_HEADS + h)),
                  pl.BlockSpec((1, c, LANES), lambda b, h, s: (b, blk(s), h)),
                  pl.BlockSpec((1, c, LANES), lambda b, h, s: (b, blk(s), HG_HEADS + h)),
                  pl.BlockSpec((1, 1, 1, LANES), lambda b, h, s: (d, h, 0, 0)),
                  pl.BlockSpec(lmat.shape, lambda b, h, s: (0, 0)),
                  pl.BlockSpec(masks.shape, lambda b, h, s: (0, 0, 0))],
        out_specs=pl.BlockSpec((1, c, LANES), lambda b, h, s: (b, blk(s), h)),
        out_shape=jax.ShapeDtypeStruct((bsz, rows, HG_WIDTH), BF16),
        scratch_shapes=[pltpu.VMEM((HG_HEAD_DIM, HG_HEAD_DIM), F32)],
        compiler_params=_cparams(("parallel", "parallel", "arbitrary")),
        name="hgrn_bwd" if d else "hgrn_fwd",
    )(z, iqg, iqg, lb4, lmat, masks)


def _post_kernel(da_ref, of_ref, ob_ref, g_ref, x_ref, mod_ref, hgn_ref, wout_ref, npost_ref, npre_ref,
                 wrh_ref, wrl_ref, x1_ref, h2_ref, aff_ref, *, d):
    b = pl.program_id(0)
    o = of_ref[0].astype(F32) + ob_ref[0].astype(F32)
    parts = []
    for h in range(HG_HEADS):
        oh = o[:, h * LANES:(h + 1) * LANES]
        ms = jnp.mean(oh * oh, axis=-1, keepdims=True)
        parts.append(oh * lax.rsqrt(ms + NORM_EPS) * hgn_ref[...])
    g = g_ref[0].astype(F32)
    ohn = jnp.concatenate(parts, axis=-1) * (g * jax.nn.sigmoid(g))
    mixed = jnp.concatenate([da_ref[0], ohn.astype(BF16)], axis=-1)
    y = _dot(mixed, wout_ref[...])
    yn = y * lax.rsqrt(jnp.mean(y * y, axis=-1, keepdims=True) + NORM_EPS) * npost_ref[...]
    gt1 = mod_ref[pl.ds(b, 1), 2 * d:3 * d]
    sh2 = mod_ref[pl.ds(b, 1), 3 * d:4 * d]
    sc2 = mod_ref[pl.ds(b, 1), 4 * d:5 * d]
    x1 = x_ref[0] + gt1 * yn
    x1_ref[0] = x1
    h2 = x1 * lax.rsqrt(jnp.mean(x1 * x1, axis=-1, keepdims=True) + NORM_EPS) * npre_ref[...]
    h2 = h2 * (1.0 + sc2) + sh2
    h2_ref[0] = h2
    hh = h2.astype(BF16)
    hl = (h2 - hh.astype(F32)).astype(BF16)
    logits = _dot_nt(wrh_ref[...], hh) + (_dot_nt(wrh_ref[...], hl) + _dot_nt(wrl_ref[...], hh))
    mx = jnp.max(logits, axis=0, keepdims=True)
    e = jnp.exp(logits - mx)
    aff_ref[0] = e / jnp.sum(e, axis=0, keepdims=True)


def _post_call(out_da, o_fwd, o_bwd, iqg, x, mod, hg_norm, w_out, npost, npre, wr_hi, wr_lo, tm):
    bsz, t, d = x.shape
    kern = functools.partial(_post_kernel, d=d)
    row = lambda b, i: (0, 0)
    return pl.pallas_call(
        kern,
        grid=(bsz, t // tm),
        in_specs=[pl.BlockSpec((1, tm, DA_WIDTH), lambda b, i: (b, i, 0)),
                  pl.BlockSpec((1, tm, HG_WIDTH), lambda b, i: (b, i, 0)),
                  pl.BlockSpec((1, tm, HG_WIDTH), lambda b, i: (b, i, 0)),
                  pl.BlockSpec((1, tm, HG_WIDTH), lambda b, i: (b, i, 2)),
                  pl.BlockSpec((1, tm, d), lambda b, i: (b, i, 0)),
                  pl.BlockSpec(mod.shape, row),
                  pl.BlockSpec((1, LANES), row),
                  pl.BlockSpec(w_out.shape, row),
                  pl.BlockSpec((1, d), row),
                  pl.BlockSpec((1, d), row),
                  pl.BlockSpec(wr_hi.shape, row),
                  pl.BlockSpec(wr_lo.shape, row)],
        out_specs=[pl.BlockSpec((1, tm, d), lambda b, i: (b, i, 0)),
                   pl.BlockSpec((1, tm, d), lambda b, i: (b, i, 0)),
                   pl.BlockSpec((1, N_EXPERTS, tm), lambda b, i: (b, 0, i))],
        out_shape=[jax.ShapeDtypeStruct((bsz, t, d), F32),
                   jax.ShapeDtypeStruct((bsz, t, d), F32),
                   jax.ShapeDtypeStruct((bsz, N_EXPERTS, t), F32)],
        compiler_params=_cparams(("parallel", "arbitrary")),
        name="post",
    )(out_da, o_fwd, o_bwd, iqg, x, mod, hg_norm, w_out, npost, npre, wr_hi, wr_lo)


TOPK_BLOCK = 256
TOPK_COUNT_BLOCK = 1024


def _topk_kernel(aff_ref, utri_ref, idx_ref, gate_ref, selc_ref, cnt_ref, cnt_smem, acci_ref, accg_ref,
                 *, t, cap):
    ne = N_EXPERTS
    cb = TOPK_BLOCK
    nblk = t // cb
    kb = min(TOPK_COUNT_BLOCK, t)

    def as_float(bits):
        return lax.bitcast_convert_type(bits, F32)

    def count_ge(thr):
        thr_b = jnp.broadcast_to(thr, (ne, kb))

        def body(i, acc):
            a = aff_ref[0, :, pl.ds(pl.multiple_of(i * kb, kb), kb)]
            return acc + jnp.where(a >= thr_b, 1, 0)
        acc = lax.fori_loop(0, t // kb, body, jnp.zeros((ne, kb), jnp.int32))
        return jnp.sum(acc, axis=1, keepdims=True)

    def bit_body(k, tau):
        cand = tau | (jnp.int32(1) << (29 - k))
        return jnp.where(count_ge(as_float(cand)) >= cap, cand, tau)
    tau = lax.fori_loop(0, 30, bit_body, jnp.zeros((ne, 1), jnp.int32))
    tau_f = as_float(tau)
    nxt_f = as_float(tau + 1)

    n_gt = count_ge(nxt_f)
    need = (cap - n_gt).astype(F32)

    utri = utri_ref[...]

    def sel_body(i, carry):
        ceq, csel = carry
        a = aff_ref[0, :, pl.ds(pl.multiple_of(i * cb, cb), cb)]
        gt = a >= nxt_f
        eq = (a >= tau_f) & jnp.logical_not(gt)
        eq_rank = _dot(jnp.where(eq, 1.0, 0.0).astype(BF16), utri) + ceq
        sel = gt | (eq & (eq_rank <= need))
        c_incl = _dot(jnp.where(sel, 1.0, 0.0).astype(BF16), utri) + csel
        selc_ref[:, pl.ds(pl.multiple_of(i * cb, cb), cb)] = jnp.where(sel, c_incl, 0.0)
        cnt_ref[i] = jnp.broadcast_to(csel, (ne, LANES)).astype(jnp.int32)
        return eq_rank[:, cb - 1:cb], c_incl[:, cb - 1:cb]
    zero = jnp.zeros((ne, 1), F32)
    _, c_all = lax.fori_loop(0, nblk, sel_body, (zero, zero))
    cnt_ref[nblk] = jnp.broadcast_to(c_all, (ne, LANES)).astype(jnp.int32)

    def to_smem(i, _):
        blk = cnt_ref[i]
        for e in range(ne):
            cnt_smem[i * ne + e] = blk[e, 0]
        return 0
    lax.fori_loop(0, nblk + 1, to_smem, 0)

    n_pt = cap // LANES
    slot = lax.broadcasted_iota(jnp.int32, (LANES, cb), 0)
    for e in range(ne):
        acci_ref[...] = jnp.zeros(acci_ref.shape, F32)
        accg_ref[...] = jnp.zeros(accg_ref.shape, F32)

        def slot_body(i, _):
            cols = pl.ds(pl.multiple_of(i * cb, cb), cb)
            cc = jnp.broadcast_to(selc_ref[e:e + 1, cols], (LANES, cb))
            aa = jnp.broadcast_to(aff_ref[0, e:e + 1, cols], (LANES, cb))
            tt = (lax.broadcasted_iota(jnp.int32, (LANES, cb), 1) + i * cb).astype(F32)
            first = cnt_smem[i * ne + e]
            last = cnt_smem[(i + 1) * ne + e] - 1

            def tile_body(j, _):
                hit = cc == (slot + (j * LANES + 1)).astype(F32)
                ti = jnp.where(hit, tt, 0.0)
                gi = jnp.where(hit, aa, 0.0)
                for c0 in range(0, cb, LANES):
                    acci_ref[j] = acci_ref[j] + ti[:, c0:c0 + LANES]
                    accg_ref[j] = accg_ref[j] + gi[:, c0:c0 + LANES]
                return 0
            lax.fori_loop(first // LANES, last // LANES + 1, tile_body, 0)
            return 0
        lax.fori_loop(0, nblk, slot_body, 0)
        for j in range(n_pt):
            idx_ref[0, e:e + 1, j * LANES:(j + 1) * LANES] = (
                jnp.sum(acci_ref[j].T, axis=0, keepdims=True).astype(jnp.int32))
            gate_ref[0, e:e + 1, j * LANES:(j + 1) * LANES] = jnp.sum(accg_ref[j].T, axis=0, keepdims=True)


def _topk_call(aff, cap):
    bsz, ne, t = aff.shape
    cb = TOPK_BLOCK
    utri = jnp.asarray(np.triu(np.ones((cb, cb), np.float32)), BF16)
    kern = functools.partial(_topk_kernel, t=t, cap=cap)
    return pl.pallas_call(
        kern,
        grid=(bsz,),
        in_specs=[pl.BlockSpec((1, ne, t), lambda b: (b, 0, 0)),
                  pl.BlockSpec((cb, cb), lambda b: (0, 0))],
        out_specs=[pl.BlockSpec((1, ne, cap), lambda b: (b, 0, 0)),
                   pl.BlockSpec((1, ne, cap), lambda b: (b, 0, 0))],
        out_shape=[jax.ShapeDtypeStruct((bsz, ne, cap), jnp.int32),
                   jax.ShapeDtypeStruct((bsz, ne, cap), F32)],
        scratch_shapes=[pltpu.VMEM((ne, t), F32),
                        pltpu.VMEM((t // cb + 1, ne, LANES), jnp.int32),
                        pltpu.SMEM(((t // cb + 1) * ne,), jnp.int32),
                        pltpu.VMEM((cap // LANES, LANES, LANES), F32),
                        pltpu.VMEM((cap // LANES, LANES, LANES), F32)],
        compiler_params=_cparams(("arbitrary",)),
        name="topk",
    )(aff, utri)


def _moe_kernel(idx_ref, h2_ref, gate_ref, wg_ref, wu_ref, wd_ref, f_ref,
                f_scr, xin, xbf, yacc, gsem, osem, *, t, cap, ne, nf, zr):
    b = pl.program_id(0)
    e = pl.program_id(1)
    fc = pl.program_id(2)
    base = (b * ne + e) * cap

    @pl.when((e == 0) & (fc == 0))
    def _():
        def zbody(i, _):
            f_scr[pl.ds(i * zr, zr), :] = jnp.zeros((zr, f_scr.shape[1]), F32)
            return 0
        lax.fori_loop(0, t // zr, zbody, 0)

    def row_copy(p):
        tok = idx_ref[base + p]
        return pltpu.make_async_copy(h2_ref.at[b, pl.ds(tok, 1), :], xin.at[pl.ds(p, 1), :], gsem)

    @pl.when(fc == 0)
    def _():
        def gstart(p, _):
            row_copy(p).start()
            return 0
        lax.fori_loop(0, cap, gstart, 0, unroll=8)
        pltpu.make_async_copy(h2_ref.at[b, pl.ds(0, cap), :], xin, gsem).wait()
        xbf[...] = xin[...].astype(BF16)

    x = xbf[...]
    g = _dot(x, wg_ref[0].astype(BF16))
    u = _dot(x, wu_ref[0].astype(BF16))
    hmid = (g * jax.nn.sigmoid(g) * u).astype(BF16)
    contrib = _dot(hmid, wd_ref[0].astype(BF16))

    @pl.when(fc == 0)
    def _():
        yacc[...] = contrib

    @pl.when(fc != 0)
    def _():
        yacc[...] = yacc[...] + contrib

    @pl.when(fc == nf - 1)
    def _():
        yacc[...] = yacc[...] * gate_ref[0, 0]

        def sbody(g, _):
            p0 = g * SCATTER_GROUP
            toks = [idx_ref[base + p0 + i] for i in range(SCATTER_GROUP)]
            yr = [yacc[pl.ds(p0 + i, 1), :] for i in range(SCATTER_GROUP)]
            fr = [f_scr[pl.ds(toks[i], 1), :] for i in range(SCATTER_GROUP)]
            for i in range(SCATTER_GROUP):
                f_scr[pl.ds(toks[i], 1), :] = fr[i] + yr[i]
            return 0
        lax.fori_loop(0, cap // SCATTER_GROUP, sbody, 0)

    @pl.when((e == ne - 1) & (fc == nf - 1))
    def _():
        cp = pltpu.make_async_copy(f_scr, f_ref.at[b], osem)
        cp.start()
        cp.wait()


def _moe_call(idx_flat, h2, gate4, w_gate, w_up, w_down, cap, tf):
    bsz, t, d = h2.shape
    ne, _, ff = w_gate.shape
    nf = ff // tf
    kern = functools.partial(_moe_kernel, t=t, cap=cap, ne=ne, nf=nf, zr=256)
    grid_spec = pltpu.PrefetchScalarGridSpec(
        num_scalar_prefetch=1,
        grid=(bsz, ne, nf),
        in_specs=[pl.BlockSpec(memory_space=pl.ANY),
                  pl.BlockSpec((1, 1, cap, 1), lambda b, e, f, idx: (b, e, 0, 0)),
                  pl.BlockSpec((1, d, tf), lambda b, e, f, idx: (e, 0, f)),
                  pl.BlockSpec((1, d, tf), lambda b, e, f, idx: (e, 0, f)),
                  pl.BlockSpec((1, tf, d), lambda b, e, f, idx: (e, f, 0))],
        out_specs=pl.BlockSpec(memory_space=pl.ANY),
        scratch_shapes=[pltpu.VMEM((t, d), F32),
                        pltpu.VMEM((cap, d), F32),
                        pltpu.VMEM((cap, d), BF16),
                        pltpu.VMEM((cap, d), F32),
                        pltpu.SemaphoreType.DMA(()),
                        pltpu.SemaphoreType.DMA(())],
    )
    return pl.pallas_call(
        kern,
        grid_spec=grid_spec,
        out_shape=jax.ShapeDtypeStruct((bsz, t, d), F32),
        compiler_params=_cparams(("arbitrary", "arbitrary", "arbitrary")),
        name="moe",
    )(idx_flat, h2, gate4, w_gate, w_up, w_down)


def _final_kernel(x1_ref, f_ref, mod_ref, n_ref, o_ref, *, d):
    b = pl.program_id(0)
    f = f_ref[0]
    fn = f * lax.rsqrt(jnp.mean(f * f, axis=-1, keepdims=True) + NORM_EPS) * n_ref[...]
    gt2 = mod_ref[pl.ds(b, 1), 5 * d:6 * d]
    o_ref[0] = x1_ref[0] + gt2 * fn


def _final_call(x1, f, mod, npost, tm):
    bsz, t, d = x1.shape
    kern = functools.partial(_final_kernel, d=d)
    return pl.pallas_call(
        kern,
        grid=(bsz, t // tm),
        in_specs=[pl.BlockSpec((1, tm, d), lambda b, i: (b, i, 0)),
                  pl.BlockSpec((1, tm, d), lambda b, i: (b, i, 0)),
                  pl.BlockSpec(mod.shape, lambda b, i: (0, 0)),
                  pl.BlockSpec((1, d), lambda b, i: (0, 0))],
        out_specs=pl.BlockSpec((1, tm, d), lambda b, i: (b, i, 0)),
        out_shape=jax.ShapeDtypeStruct((bsz, t, d), F32),
        compiler_params=_cparams(("parallel", "arbitrary")),
        name="final",
    )(x1, f, mod, npost)


def _rope_tables(t, ctx_len):
    half = DA_HEAD_DIM // 2
    inv_freq = 1.0 / (ROPE_THETA ** (jnp.arange(0, half, 2, dtype=F32) / half))
    pos = jnp.arange(t, dtype=jnp.int32)
    r = (pos // GRID_W).astype(F32)
    col = (pos % GRID_W).astype(F32)
    ang_r = r[:, None] * inv_freq
    ang_c = col[:, None] * inv_freq
    ang = jnp.concatenate([ang_r, ang_r, ang_c, ang_c], axis=-1)
    cos = jnp.concatenate([jnp.cos(ang), jnp.ones((ctx_len, DA_HEAD_DIM), F32)], axis=0)
    sin = jnp.concatenate([jnp.sin(ang), jnp.zeros((ctx_len, DA_HEAD_DIM), F32)], axis=0)
    return jnp.tile(cos, (1, 2)), jnp.tile(sin, (1, 2))


def _rot_columns(w):
    k, n = w.shape
    w4 = w.reshape(k, n // DA_HEAD_DIM, 4, DA_HEAD_DIM // 4)
    return jnp.stack([-w4[:, :, 1], w4[:, :, 0], -w4[:, :, 3], w4[:, :, 2]], axis=2).reshape(k, n)


def kernel(x, c, ctx, c_ctx, w_ada, b_ada, norm_pre_mix, norm_post_mix, norm_pre_ffn, norm_post_ffn, w_in,
           da_lambda_q1, da_lambda_k1, da_lambda_q2, da_lambda_k2, da_subln, hg_lower_bound, hg_norm, w_out,
           w_router, w_gate, w_up, w_down):
    bsz, t, d = x.shape
    ctx_len = ctx.shape[1]
    assert bsz <= 2 and d == 1024 and w_ada.shape[0] == 1
    cap = CAPACITY_FACTOR * t // N_EXPERTS
    lambda_init = 0.8 - 0.6 * math.exp(-0.3 * 0)

    lam = (jnp.exp(jnp.sum(da_lambda_q1[0].astype(F32) * da_lambda_k1[0].astype(F32)))
           - jnp.exp(jnp.sum(da_lambda_q2[0].astype(F32) * da_lambda_k2[0].astype(F32))) + lambda_init)
    lam = lam.reshape(1, 1)
    lb = jnp.cumsum(jax.nn.softmax(hg_lower_bound.astype(F32), axis=0), axis=0)[0]
    lb4 = lb.reshape(2, HG_HEADS, 1, HG_HEAD_DIM)

    wi = w_in[0]
    scale = DA_HEAD_DIM ** -0.5 * math.log2(math.e)
    wq = wi[:, 0:DA_WIDTH] * scale
    wk = wi[:, DA_WIDTH:2 * DA_WIDTH]
    w_all = jnp.concatenate([wq, wk, wi[:, 2 * DA_WIDTH:], _rot_columns(wq), _rot_columns(wk)],
                            axis=1).astype(BF16)
    cos, sin = _rope_tables(t, ctx_len)

    cmat = jnp.concatenate([c, c_ctx[None, :], jnp.zeros((8 - bsz - 1, d), F32)], axis=0)
    mod = _mod_call(cmat, w_ada[0], b_ada[0][None, :])

    qkv, z, iqg = _inproj_call(x, ctx, mod, norm_pre_mix[0][None, :], w_all, cos, sin)

    tq = min(ATTN_TQ, t)
    nk = t + ctx_len
    tk = ATTN_TK if nk % ATTN_TK == 0 else ctx_len
    out_da = _attn_call(lam, qkv, da_subln[0][None, :], t, tq, tk, 1.0 - lambda_init)
    o_fwd = _hgrn_call(z, iqg, lb4, ctx_len, 0)
    o_bwd = _hgrn_call(z, iqg, lb4, ctx_len, 1)

    wr = w_router[0].T
    wr_hi = wr.astype(BF16)
    wr_lo = (wr - wr_hi.astype(F32)).astype(BF16)
    x1, h2, aff = _post_call(out_da, o_fwd, o_bwd, iqg, x, mod, hg_norm[0][None, :], w_out[0].astype(BF16),
                             norm_post_mix[0][None, :], norm_pre_ffn[0][None, :], wr_hi, wr_lo, 256)
    idx, gate = _topk_call(aff, cap)
    f = _moe_call(idx.reshape(-1), h2, gate.reshape(bsz, N_EXPERTS, cap, 1), w_gate[0], w_up[0], w_down[0],
                  cap, 256)
    return _final_call(x1, f, mod, norm_post_ffn[0][None, :], 256)
```

```python
import functools
import math

import jax
import jax.numpy as jnp
import numpy as np
from jax import lax
from jax.experimental import pallas as pl
from jax.experimental.pallas import tpu as pltpu

F32 = jnp.float32
BF16 = jnp.bfloat16

GRID_W = 64
DA_HEADS = 4
DA_HEAD_DIM = 64
DA_V_DIM = 128
DA_WIDTH = 512
HG_HEADS = 4
HG_HEAD_DIM = 128
HG_WIDTH = 512
N_EXPERTS = 16
CAPACITY_FACTOR = 2
ROPE_THETA = 10000.0
NORM_EPS = 1e-6
LANES = 128
VMEM_LIMIT = 60 * 1024 * 1024
ATTN_TQ = 1024
ATTN_TK = 768
ATTN_ROW_BLOCK = 128
SCATTER_GROUP = 8


def _cparams(sem, vmem=VMEM_LIMIT):
    return pltpu.CompilerParams(dimension_semantics=sem, vmem_limit_bytes=vmem)


def _dot(a, b):
    return jnp.dot(a, b, preferred_element_type=F32)


def _dot_nt(a, b):
    return lax.dot_general(a, b, (((1,), (1,)), ((), ())), preferred_element_type=F32)


def _dot_tn(a, b):
    return lax.dot_general(a, b, (((0,), (0,)), ((), ())), preferred_element_type=F32)


def _split3(x):
    hi = x.astype(BF16)
    r1 = x - hi.astype(F32)
    mid = r1.astype(BF16)
    lo = (r1 - mid.astype(F32)).astype(BF16)
    return hi, mid, lo


def _mod_kernel(s_ref, w_ref, b_ref, o_ref):
    s = s_ref[...]
    s = s * jax.nn.sigmoid(s)
    hi, mid, lo = _split3(s)
    w = w_ref[...]
    whi, wmid, wlo = _split3(w)
    acc = _dot(hi, whi) + (_dot(hi, wmid) + _dot(mid, whi)) + (_dot(hi, wlo) + _dot(mid, wmid) + _dot(lo, whi))
    o_ref[...] = acc + b_ref[...]


def _mod_call(cmat, w_ada, b_ada):
    d = cmat.shape[1]
    n = w_ada.shape[1]
    tn = 1536 if n % 1536 == 0 else n
    return pl.pallas_call(
        _mod_kernel,
        grid=(n // tn,),
        in_specs=[pl.BlockSpec((8, d), lambda j: (0, 0)),
                  pl.BlockSpec((d, tn), lambda j: (0, j)),
                  pl.BlockSpec((1, tn), lambda j: (0, j))],
        out_specs=pl.BlockSpec((8, tn), lambda j: (0, j)),
        out_shape=jax.ShapeDtypeStruct((8, n), F32),
        compiler_params=_cparams(("arbitrary",)),
        name="mod",
    )(cmat, w_ada, b_ada)


def _inproj_kernel(x_ref, ctx_ref, mod_ref, g_ref, w_ref, cos_ref, sin_ref, qkv_ref, z_ref, iqg_ref, *, n_lat, d):
    b = pl.program_id(0)
    i = pl.program_id(1)
    is_ctx = i == n_lat
    xb = jnp.where(is_ctx, ctx_ref[0], x_ref[0])
    row = jnp.where(is_ctx, 2, b)
    sh = mod_ref[pl.ds(row, 1), 0:d]
    sc = mod_ref[pl.ds(row, 1), d:2 * d]
    ms = jnp.mean(xb * xb, axis=-1, keepdims=True)
    h = xb * lax.rsqrt(ms + NORM_EPS) * g_ref[...]
    h = h * (1.0 + sc) + sh
    p = _dot(h.astype(BF16), w_ref[...])
    cos = jnp.tile(cos_ref[...], (1, 8))
    sin = jnp.tile(sin_ref[...], (1, 8))
    qk = p[:, 0:1024] * cos + p[:, 4096:5120] * sin
    qkv_ref[0, :, 0:1024] = qk.astype(BF16)
    qkv_ref[0, :, 1024:1536] = p[:, 1024:1536].astype(BF16)
    z_ref[0] = p[:, 1536:2560]
    iqg_ref[0] = p[:, 2560:4096].astype(BF16)


def _inproj_call(x, ctx, mod, g, w, cos, sin):
    bsz, t, d = x.shape
    tm = ctx.shape[1]
    n_lat = t // tm
    rows = t + tm
    kern = functools.partial(_inproj_kernel, n_lat=n_lat, d=d)
    return pl.pallas_call(
        kern,
        grid=(bsz, n_lat + 1),
        in_specs=[pl.BlockSpec((1, tm, d), lambda b, i: (b, jnp.minimum(i, n_lat - 1), 0)),
                  pl.BlockSpec((1, tm, d), lambda b, i: (b, 0, 0)),
                  pl.BlockSpec(mod.shape, lambda b, i: (0, 0)),
                  pl.BlockSpec((1, d), lambda b, i: (0, 0)),
                  pl.BlockSpec(w.shape, lambda b, i: (0, 0)),
                  pl.BlockSpec((tm, LANES), lambda b, i: (i, 0)),
                  pl.BlockSpec((tm, LANES), lambda b, i: (i, 0))],
        out_specs=[pl.BlockSpec((1, tm, 1536), lambda b, i: (b, i, 0)),
                   pl.BlockSpec((1, tm, 1024), lambda b, i: (b, i, 0)),
                   pl.BlockSpec((1, tm, 1536), lambda b, i: (b, i, 0))],
        out_shape=[jax.ShapeDtypeStruct((bsz, rows, 1536), BF16),
                   jax.ShapeDtypeStruct((bsz, rows, 1024), F32),
                   jax.ShapeDtypeStruct((bsz, rows, 1536), BF16)],
        compiler_params=_cparams(("parallel", "arbitrary")),
        name="inproj",
    )(x, ctx, mod, g, w, cos, sin)


def _attn_kernel(lam_ref, q_ref, k_ref, v_ref, subln_ref, o_ref, qs_ref, sa_ref, sb_ref, pa_ref, pb_ref, m_ref,
                 acc_ref, *, tq, tk, nchunk, out_scale):
    q = q_ref[0]
    lane = lax.broadcasted_iota(jnp.int32, q.shape, 1)
    zero = jnp.zeros_like(q)
    qs_ref[0:tq, :] = jnp.where(lane < DA_HEAD_DIM, q, zero)
    qs_ref[tq:2 * tq, :] = jnp.where(lane >= DA_HEAD_DIM, q, zero)

    m_ref[...] = jnp.full(m_ref.shape, -1e30, F32)
    acc_ref[...] = jnp.zeros(acc_ref.shape, F32)

    def scores(j, s_ref):
        kj = k_ref[0, pl.ds(pl.multiple_of(j * tk, tk), tk), :]
        s_ref[...] = _dot_nt(qs_ref[...], kj)

    def update(j, s_ref, p_ref):
        vj = v_ref[0, pl.ds(pl.multiple_of(j * tk, tk), tk), :]
        vext = jnp.concatenate([vj, jnp.ones_like(vj)], axis=-1)
        m = m_ref[...]
        m_new = jnp.maximum(m, jnp.broadcast_to(jnp.max(s_ref[...], axis=-1, keepdims=True), m.shape))
        alpha = jnp.exp2(m - m_new)
        m_ref[...] = m_new
        rb = min(ATTN_ROW_BLOCK, 2 * tq)
        for r0 in range(0, 2 * tq, rb):
            rows = slice(r0, r0 + rb)
            mb = m_ref[rows, :]
            for c0 in range(0, tk, LANES):
                p_ref[rows, c0:c0 + LANES] = jnp.exp2((s_ref[rows, c0:c0 + LANES] - mb).astype(BF16))
        acc_ref[...] = jnp.tile(alpha, (1, 2)) * acc_ref[...] + _dot(p_ref[...], vext)

    npairs = (nchunk - 1) // 2
    scores(0, sa_ref)

    def body(i, _):
        scores(2 * i + 1, sb_ref)
        update(2 * i, sa_ref, pa_ref)
        scores(2 * i + 2, sa_ref)
        update(2 * i + 1, sb_ref, pb_ref)
        return 0
    lax.fori_loop(0, npairs, body, 0)
    if nchunk - 1 == 2 * npairs:
        update(nchunk - 1, sa_ref, pa_ref)
    else:
        scores(nchunk - 1, sb_ref)
        update(nchunk - 2, sa_ref, pa_ref)
        update(nchunk - 1, sb_ref, pb_ref)

    r = 1.0 / acc_ref[:, LANES:LANES + 1]
    acc = acc_ref[:, 0:LANES]
    o = acc[0:tq] * r[0:tq] - lam_ref[0, 0] * (acc[tq:2 * tq] * r[tq:2 * tq])
    ms = jnp.mean(o * o, axis=-1, keepdims=True)
    o = o * lax.rsqrt(ms + NORM_EPS) * subln_ref[...] * out_scale
    o_ref[0] = o.astype(o_ref.dtype)


def _attn_call(lam, qkv, subln, t, tq, tk, out_scale):
    bsz, rows, _ = qkv.shape
    kern = functools.partial(_attn_kernel, tq=tq, tk=tk, nchunk=rows // tk, out_scale=out_scale)
    return pl.pallas_call(
        kern,
        grid=(bsz, DA_HEADS, t // tq),
        in_specs=[pl.BlockSpec(memory_space=pltpu.SMEM),
                  pl.BlockSpec((1, tq, LANES), lambda b, h, i: (b, i, h)),
                  pl.BlockSpec((1, rows, LANES), lambda b, h, i: (b, 0, DA_HEADS + h)),
                  pl.BlockSpec((1, rows, LANES), lambda b, h, i: (b, 0, 2 * DA_HEADS + h)),
                  pl.BlockSpec((1, LANES), lambda b, h, i: (0, 0))],
        out_specs=pl.BlockSpec((1, tq, LANES), lambda b, h, i: (b, i, h)),
        out_shape=jax.ShapeDtypeStruct((bsz, t, DA_WIDTH), BF16),
        scratch_shapes=[pltpu.VMEM((2 * tq, LANES), BF16),
                        pltpu.VMEM((2 * tq, tk), F32),
                        pltpu.VMEM((2 * tq, tk), F32),
                        pltpu.VMEM((2 * tq, tk), BF16),
                        pltpu.VMEM((2 * tq, tk), BF16),
                        pltpu.VMEM((2 * tq, LANES), F32),
                        pltpu.VMEM((2 * tq, 2 * LANES), F32)],
        compiler_params=_cparams(("parallel", "parallel", "arbitrary")),
        name="attn",
    )(lam, qkv, qkv, qkv, subln)


HGRN_SUB_LEVELS = 3


def _hgrn_consts(c, d):
    nl = int(math.log2(c))
    t = np.arange(c)
    pi = t if d == 0 else c - 1 - t
    pr, pu = pi[:, None], pi[None, :]
    lm = np.zeros(((HGRN_SUB_LEVELS + 1) * c, c), np.float32)
    for lg in range(HGRN_SUB_LEVELS):
        same = (pr >> lg) == (pu >> lg)
        sec = ((pr >> lg) & 1) == 1
        lm[lg * c:(lg + 1) * c] = same & np.where(sec, pu <= pr, pu > pr)
    lm[HGRN_SUB_LEVELS * c:] = pu <= pr
    masks = np.zeros((nl + 1, c, c), np.float32)
    for lg in range(nl):
        masks[lg] = (t[:, None] >> (lg + 1)) == (t[None, :] >> (lg + 1))
    masks[nl] = np.eye(c)
    return jnp.asarray(lm, BF16), jnp.asarray(masks, F32), nl


def _hgrn_kernel(z_ref, i_ref, q_ref, lb_ref, lmat_ref, mask_ref, o_ref, st_ref, *, c, nl, d):
    s = pl.program_id(1)
    heads = range(HG_HEADS)

    @pl.when(s == 0)
    def _():
        st_ref[...] = jnp.zeros(st_ref.shape, F32)

    def col(h):
        return slice(h * LANES, (h + 1) * LANES)

    f = [lb_ref[0, h] + (1.0 - lb_ref[0, h]) * jax.nn.sigmoid(z_ref[0, :, col(h)]) for h in heads]
    lf = [jnp.log(f[h]) for h in heads]
    kk = [1.0 - f[h] for h in heads]
    v = [i_ref[0, :, col(h)] for h in heads]
    q = [q_ref[0, :, col(h)].astype(F32) for h in heads]
    x = []
    for h in heads:
        hi = lf[h].astype(BF16)
        lo = (lf[h] - hi.astype(F32)).astype(BF16)
        x2 = _dot(lmat_ref[...], jnp.concatenate([hi, lo], axis=-1))
        x.append(x2[:, 0:LANES] + x2[:, LANES:2 * LANES])
    cum = [x[h][HGRN_SUB_LEVELS * c:] for h in heads]
    last = c - 1 if d == 0 else 0
    bend = [cum[h][last:last + 1] for h in heads]

    row = lax.broadcasted_iota(jnp.int32, (c, LANES), 0)
    scores = [mask_ref[nl] * _dot_nt(q[h].astype(BF16), kk[h].astype(BF16)) for h in heads]
    for lg in range(nl):
        hs = 1 << lg
        sec = ((row >> lg) & 1) == (1 - d)
        for h in heads:
            if lg < HGRN_SUB_LEVELS:
                xl = x[h][lg * c:(lg + 1) * c]
            else:
                mid = hs - 1 if d == 0 else hs
                c3 = cum[h].reshape(c // (2 * hs), 2 * hs, LANES)
                ref = jnp.broadcast_to(c3[:, mid:mid + 1, :], c3.shape).reshape(c, LANES)
                xl = -jnp.abs(cum[h] - ref)
            g = jnp.exp(xl)
            qt = jnp.where(sec, q[h] * g, 0.0).astype(BF16)
            kt = jnp.where(sec, 0.0, kk[h] * g).astype(BF16)
            scores[h] = scores[h] + mask_ref[lg] * _dot_nt(qt, kt)
    for h in heads:
        st = st_ref[h]
        o = (_dot_nt((q[h] * jnp.exp(cum[h])).astype(BF16), st.astype(BF16))
             + _dot(scores[h].astype(BF16), v[h]))
        kb = (kk[h] * jnp.exp(bend[h] - cum[h])).astype(BF16)
        st_ref[h] = st * jnp.exp(bend[h]) + _dot_tn(v[h], kb)
        o_ref[0, :, col(h)] = o.astype(o_ref.dtype)


def _hgrn_call(z, iqg, lb4, c, d):
    bsz, rows, _ = z.shape
    nb = rows // c - 1
    lmat, masks, nl = _hgrn_consts(c, d)

    def blk(s):
        return jnp.where(s == 0, nb, s - 1 if d == 0 else nb - s)

    kern = functools.partial(_hgrn_kernel, c=c, nl=nl, d=d)
    return pl.pallas_call(
        kern,
        grid=(bsz, nb + 1),
        in_specs=[pl.BlockSpec((1, c, HG_WIDTH), lambda b, s: (b, blk(s), d)),
                  pl.BlockSpec((1, c, HG_WIDTH), lambda b, s: (b, blk(s), 0)),
                  pl.BlockSpec((1, c, HG_WIDTH), lambda b, s: (b, blk(s), 1)),
                  pl.BlockSpec((1, HG_HEADS, 1, LANES), lambda b, s: (d, 0, 0, 0)),
                  pl.BlockSpec(lmat.shape, lambda b, s: (0, 0)),
                  pl.BlockSpec(masks.shape, lambda b, s: (0, 0, 0))],
        out_specs=pl.BlockSpec((1, c, HG_WIDTH), lambda b, s: (b, blk(s), 0)),
        out_shape=jax.ShapeDtypeStruct((bsz, rows, HG_WIDTH), BF16),
        scratch_shapes=[pltpu.VMEM((HG_HEADS, HG_HEAD_DIM, HG_HEAD_DIM), F32)],
        compiler_params=_cparams(("parallel", "arbitrary")),
        name="hgrn_bwd" if d else "hgrn_fwd",
    )(z, iqg, iqg, lb4, lmat, masks)


def _post_kernel(da_ref, of_ref, ob_ref, g_ref, x_ref, mod_ref, hgn_ref, wout_ref, npost_ref, npre_ref,
                 wrh_ref, wrl_ref, x1_ref, h2_ref, aff_ref, *, d):
    b = pl.program_id(0)
    o = of_ref[0].astype(F32) + ob_ref[0].astype(F32)
    parts = []
    for h in range(HG_HEADS):
        oh = o[:, h * LANES:(h + 1) * LANES]
        ms = jnp.mean(oh * oh, axis=-1, keepdims=True)
        parts.append(oh * lax.rsqrt(ms + NORM_EPS) * hgn_ref[...])
    g = g_ref[0].astype(F32)
    ohn = jnp.concatenate(parts, axis=-1) * (g * jax.nn.sigmoid(g))
    mixed = jnp.concatenate([da_ref[0], ohn.astype(BF16)], axis=-1)
    y = _dot(mixed, wout_ref[...])
    yn = y * lax.rsqrt(jnp.mean(y * y, axis=-1, keepdims=True) + NORM_EPS) * npost_ref[...]
    gt1 = mod_ref[pl.ds(b, 1), 2 * d:3 * d]
    sh2 = mod_ref[pl.ds(b, 1), 3 * d:4 * d]
    sc2 = mod_ref[pl.ds(b, 1), 4 * d:5 * d]
    x1 = x_ref[0] + gt1 * yn
    x1_ref[0] = x1
    h2 = x1 * lax.rsqrt(jnp.mean(x1 * x1, axis=-1, keepdims=True) + NORM_EPS) * npre_ref[...]
    h2 = h2 * (1.0 + sc2) + sh2
    h2_ref[0] = h2
    hh = h2.astype(BF16)
    hl = (h2 - hh.astype(F32)).astype(BF16)
    logits = _dot_nt(wrh_ref[...], hh) + (_dot_nt(wrh_ref[...], hl) + _dot_nt(wrl_ref[...], hh))
    mx = jnp.max(logits, axis=0, keepdims=True)
    e = jnp.exp(logits - mx)
    aff_ref[0] = e / jnp.sum(e, axis=0, keepdims=True)


def _post_call(out_da, o_fwd, o_bwd, iqg, x, mod, hg_norm, w_out, npost, npre, wr_hi, wr_lo, tm):
    bsz, t, d = x.shape
    kern = functools.partial(_post_kernel, d=d)
    row = lambda b, i: (0, 0)
    return pl.pallas_call(
        kern,
        grid=(bsz, t // tm),
        in_specs=[pl.BlockSpec((1, tm, DA_WIDTH), lambda b, i: (b, i, 0)),
                  pl.BlockSpec((1, tm, HG_WIDTH), lambda b, i: (b, i, 0)),
                  pl.BlockSpec((1, tm, HG_WIDTH), lambda b, i: (b, i, 0)),
                  pl.BlockSpec((1, tm, HG_WIDTH), lambda b, i: (b, i, 2)),
                  pl.BlockSpec((1, tm, d), lambda b, i: (b, i, 0)),
                  pl.BlockSpec(mod.shape, row),
                  pl.BlockSpec((1, LANES), row),
                  pl.BlockSpec(w_out.shape, row),
                  pl.BlockSpec((1, d), row),
                  pl.BlockSpec((1, d), row),
                  pl.BlockSpec(wr_hi.shape, row),
                  pl.BlockSpec(wr_lo.shape, row)],
        out_specs=[pl.BlockSpec((1, tm, d), lambda b, i: (b, i, 0)),
                   pl.BlockSpec((1, tm, d), lambda b, i: (b, i, 0)),
                   pl.BlockSpec((1, N_EXPERTS, tm), lambda b, i: (b, 0, i))],
        out_shape=[jax.ShapeDtypeStruct((bsz, t, d), F32),
                   jax.ShapeDtypeStruct((bsz, t, d), F32),
                   jax.ShapeDtypeStruct((bsz, N_EXPERTS, t), F32)],
        compiler_params=_cparams(("parallel", "arbitrary")),
        name="post",
    )(out_da, o_fwd, o_bwd, iqg, x, mod, hg_norm, w_out, npost, npre, wr_hi, wr_lo)


TOPK_BLOCK = 256
TOPK_COUNT_BLOCK = 1024


def _topk_kernel(aff_ref, utri_ref, idx_ref, gate_ref, selc_ref, cnt_ref, cnt_smem, acci_ref, accg_ref,
                 *, t, cap):
    ne = N_EXPERTS
    cb = TOPK_BLOCK
    nblk = t // cb
    kb = min(TOPK_COUNT_BLOCK, t)

    def as_float(bits):
        return lax.bitcast_convert_type(bits, F32)

    def count_ge(thr):
        thr_b = jnp.broadcast_to(thr, (ne, kb))

        def body(i, acc):
            a = aff_ref[0, :, pl.ds(pl.multiple_of(i * kb, kb), kb)]
            return acc + jnp.where(a >= thr_b, 1, 0)
        acc = lax.fori_loop(0, t // kb, body, jnp.zeros((ne, kb), jnp.int32))
        return jnp.sum(acc, axis=1, keepdims=True)

    def bit_body(k, tau):
        cand = tau | (jnp.int32(1) << (29 - k))
        return jnp.where(count_ge(as_float(cand)) >= cap, cand, tau)
    tau = lax.fori_loop(0, 30, bit_body, jnp.zeros((ne, 1), jnp.int32))
    tau_f = as_float(tau)
    nxt_f = as_float(tau + 1)

    n_gt = count_ge(nxt_f)
    need = (cap - n_gt).astype(F32)

    utri = utri_ref[...]

    def sel_body(i, carry):
        ceq, csel = carry
        a = aff_ref[0, :, pl.ds(pl.multiple_of(i * cb, cb), cb)]
        gt = a >= nxt_f
        eq = (a >= tau_f) & jnp.logical_not(gt)
        eq_rank = _dot(jnp.where(eq, 1.0, 0.0).astype(BF16), utri) + ceq
        sel = gt | (eq & (eq_rank <= need))
        c_incl = _dot(jnp.where(sel, 1.0, 0.0).astype(BF16), utri) + csel
        selc_ref[:, pl.ds(pl.multiple_of(i * cb, cb), cb)] = jnp.where(sel, c_incl, 0.0)
        cnt_ref[i] = jnp.broadcast_to(csel, (ne, LANES)).astype(jnp.int32)
        return eq_rank[:, cb - 1:cb], c_incl[:, cb - 1:cb]
    zero = jnp.zeros((ne, 1), F32)
    _, c_all = lax.fori_loop(0, nblk, sel_body, (zero, zero))
    cnt_ref[nblk] = jnp.broadcast_to(c_all, (ne, LANES)).astype(jnp.int32)

    def to_smem(i, _):
        blk = cnt_ref[i]
        for e in range(ne):
            cnt_smem[i * ne + e] = blk[e, 0]
        return 0
    lax.fori_loop(0, nblk + 1, to_smem, 0)

    n_pt = cap // LANES
    slot = lax.broadcasted_iota(jnp.int32, (LANES, cb), 0)
    for e in range(ne):
        acci_ref[...] = jnp.zeros(acci_ref.shape, F32)
        accg_ref[...] = jnp.zeros(accg_ref.shape, F32)

        def slot_body(i, _):
            cols = pl.ds(pl.multiple_of(i * cb, cb), cb)
            cc = jnp.broadcast_to(selc_ref[e:e + 1, cols], (LANES, cb))
            aa = jnp.broadcast_to(aff_ref[0, e:e + 1, cols], (LANES, cb))
            tt = (lax.broadcasted_iota(jnp.int32, (LANES, cb), 1) + i * cb).astype(F32)
            first = cnt_smem[i * ne + e]
            last = cnt_smem[(i + 1) * ne + e] - 1

            def tile_body(j, _):
                hit = cc == (slot + (j * LANES + 1)).astype(F32)
                ti = jnp.where(hit, tt, 0.0)
                gi = jnp.where(hit, aa, 0.0)
                for c0 in range(0, cb, LANES):
                    acci_ref[j] = acci_ref[j] + ti[:, c0:c0 + LANES]
                    accg_ref[j] = accg_ref[j] + gi[:, c0:c0 + LANES]
                return 0
            lax.fori_loop(first // LANES, last // LANES + 1, tile_body, 0)
            return 0
        lax.fori_loop(0, nblk, slot_body, 0)
        for j in range(n_pt):
            idx_ref[0, e:e + 1, j * LANES:(j + 1) * LANES] = (
                jnp.sum(acci_ref[j].T, axis=0, keepdims=True).astype(jnp.int32))
            gate_ref[0, e:e + 1, j * LANES:(j + 1) * LANES] = jnp.sum(accg_ref[j].T, axis=0, keepdims=True)


def _topk_call(aff, cap):
    bsz, ne, t = aff.shape
    cb = TOPK_BLOCK
    utri = jnp.asarray(np.triu(np.ones((cb, cb), np.float32)), BF16)
    kern = functools.partial(_topk_kernel, t=t, cap=cap)
    return pl.pallas_call(
        kern,
        grid=(bsz,),
        in_specs=[pl.BlockSpec((1, ne, t), lambda b: (b, 0, 0)),
                  pl.BlockSpec((cb, cb), lambda b: (0, 0))],
        out_specs=[pl.BlockSpec((1, ne, cap), lambda b: (b, 0, 0)),
                   pl.BlockSpec((1, ne, cap), lambda b: (b, 0, 0))],
        out_shape=[jax.ShapeDtypeStruct((bsz, ne, cap), jnp.int32),
                   jax.ShapeDtypeStruct((bsz, ne, cap), F32)],
        scratch_shapes=[pltpu.VMEM((ne, t), F32),
                        pltpu.VMEM((t // cb + 1, ne, LANES), jnp.int32),
                        pltpu.SMEM(((t // cb + 1) * ne,), jnp.int32),
                        pltpu.VMEM((cap // LANES, LANES, LANES), F32),
                        pltpu.VMEM((cap // LANES, LANES, LANES), F32)],
        compiler_params=_cparams(("arbitrary",)),
        name="topk",
    )(aff, utri)


def _moe_kernel(idx_ref, h2_ref, gate_ref, wg_ref, wu_ref, wd_ref, f_ref,
                f_scr, xin, xbf, yacc, gsem, osem, *, t, cap, ne, nf, zr):
    b = pl.program_id(0)
    e = pl.program_id(1)
    fc = pl.program_id(2)
    nb = pl.num_programs(0)
    flat = b * ne + e
    base = flat * cap
    slot = flat % 2
    last = nb * ne - 1
    nxt = jnp.minimum(flat + 1, last)
    rows_per_fc = cap // nf

    @pl.when((e == 0) & (fc == 0))
    def _():
        def zbody(i, _):
            f_scr[pl.ds(i * zr, zr), :] = jnp.zeros((zr, f_scr.shape[1]), F32)
            return 0
        lax.fori_loop(0, t // zr, zbody, 0)

    def row_copy(step, dst, p):
        tok = idx_ref[step * cap + p]
        return pltpu.make_async_copy(h2_ref.at[step // ne, pl.ds(tok, 1), :],
                                     xin.at[dst, pl.ds(p, 1), :], gsem.at[dst])

    def wait_rows(dst):
        pltpu.make_async_copy(h2_ref.at[0, pl.ds(0, cap), :], xin.at[dst], gsem.at[dst]).wait()

    @pl.when((flat == 0) & (fc == 0))
    def _():
        def gstart(p, _):
            row_copy(flat, slot, p).start()
            return 0
        lax.fori_loop(0, cap, gstart, 0, unroll=8)

    @pl.when(fc == 0)
    def _():
        wait_rows(slot)
        xbf[...] = xin[slot].astype(BF16)

    for p in range(rows_per_fc):
        row_copy(nxt, 1 - slot, fc * rows_per_fc + p).start()

    x = xbf[...]
    g = _dot(x, wg_ref[0].astype(BF16))
    u = _dot(x, wu_ref[0].astype(BF16))
    hmid = (g * jax.nn.sigmoid(g) * u).astype(BF16)
    contrib = _dot(hmid, wd_ref[0].astype(BF16))

    @pl.when(fc == 0)
    def _():
        yacc[...] = contrib

    @pl.when(fc != 0)
    def _():
        yacc[...] = yacc[...] + contrib

    @pl.when(fc == nf - 1)
    def _():
        yacc[...] = yacc[...] * gate_ref[0, 0]

        def sbody(g, _):
            p0 = g * SCATTER_GROUP
            toks = [idx_ref[base + p0 + i] for i in range(SCATTER_GROUP)]
            yr = [yacc[pl.ds(p0 + i, 1), :] for i in range(SCATTER_GROUP)]
            fr = [f_scr[pl.ds(toks[i], 1), :] for i in range(SCATTER_GROUP)]
            for i in range(SCATTER_GROUP):
                f_scr[pl.ds(toks[i], 1), :] = fr[i] + yr[i]
            return 0
        lax.fori_loop(0, cap // SCATTER_GROUP, sbody, 0)

    @pl.when((e == ne - 1) & (fc == nf - 1))
    def _():
        cp = pltpu.make_async_copy(f_scr, f_ref.at[b], osem)
        cp.start()
        cp.wait()

    @pl.when((flat == last) & (fc == nf - 1))
    def _():
        wait_rows(1 - slot)


def _moe_call(idx_flat, h2, gate4, w_gate, w_up, w_down, cap, tf):
    bsz, t, d = h2.shape
    ne, _, ff = w_gate.shape
    nf = ff // tf
    kern = functools.partial(_moe_kernel, t=t, cap=cap, ne=ne, nf=nf, zr=256)
    grid_spec = pltpu.PrefetchScalarGridSpec(
        num_scalar_prefetch=1,
        grid=(bsz, ne, nf),
        in_specs=[pl.BlockSpec(memory_space=pl.ANY),
                  pl.BlockSpec((1, 1, cap, 1), lambda b, e, f, idx: (b, e, 0, 0)),
                  pl.BlockSpec((1, d, tf), lambda b, e, f, idx: (e, 0, f)),
                  pl.BlockSpec((1, d, tf), lambda b, e, f, idx: (e, 0, f)),
                  pl.BlockSpec((1, tf, d), lambda b, e, f, idx: (e, f, 0))],
        out_specs=pl.BlockSpec(memory_space=pl.ANY),
        scratch_shapes=[pltpu.VMEM((t, d), F32),
                        pltpu.VMEM((2, cap, d), F32),
                        pltpu.VMEM((cap, d), BF16),
                        pltpu.VMEM((cap, d), F32),
                        pltpu.SemaphoreType.DMA((2,)),
                        pltpu.SemaphoreType.DMA(())],
    )
    return pl.pallas_call(
        kern,
        grid_spec=grid_spec,
        out_shape=jax.ShapeDtypeStruct((bsz, t, d), F32),
        compiler_params=_cparams(("arbitrary", "arbitrary", "arbitrary")),
        name="moe",
    )(idx_flat, h2, gate4, w_gate, w_up, w_down)


def _final_kernel(x1_ref, f_ref, mod_ref, n_ref, o_ref, *, d):
    b = pl.program_id(0)
    f = f_ref[0]
    fn = f * lax.rsqrt(jnp.mean(f * f, axis=-1, keepdims=True) + NORM_EPS) * n_ref[...]
    gt2 = mod_ref[pl.ds(b, 1), 5 * d:6 * d]
    o_ref[0] = x1_ref[0] + gt2 * fn


def _final_call(x1, f, mod, npost, tm):
    bsz, t, d = x1.shape
    kern = functools.partial(_final_kernel, d=d)
    return pl.pallas_call(
        kern,
        grid=(bsz, t // tm),
        in_specs=[pl.BlockSpec((1, tm, d), lambda b, i: (b, i, 0)),
                  pl.BlockSpec((1, tm, d), lambda b, i: (b, i, 0)),
                  pl.BlockSpec(mod.shape, lambda b, i: (0, 0)),
                  pl.BlockSpec((1, d), lambda b, i: (0, 0))],
        out_specs=pl.BlockSpec((1, tm, d), lambda b, i: (b, i, 0)),
        out_shape=jax.ShapeDtypeStruct((bsz, t, d), F32),
        compiler_params=_cparams(("parallel", "arbitrary")),
        name="final",
    )(x1, f, mod, npost)


def _rope_tables(t, ctx_len):
    half = DA_HEAD_DIM // 2
    inv_freq = 1.0 / (ROPE_THETA ** (jnp.arange(0, half, 2, dtype=F32) / half))
    pos = jnp.arange(t, dtype=jnp.int32)
    r = (pos // GRID_W).astype(F32)
    col = (pos % GRID_W).astype(F32)
    ang_r = r[:, None] * inv_freq
    ang_c = col[:, None] * inv_freq
    ang = jnp.concatenate([ang_r, ang_r, ang_c, ang_c], axis=-1)
    cos = jnp.concatenate([jnp.cos(ang), jnp.ones((ctx_len, DA_HEAD_DIM), F32)], axis=0)
    sin = jnp.concatenate([jnp.sin(ang), jnp.zeros((ctx_len, DA_HEAD_DIM), F32)], axis=0)
    return jnp.tile(cos, (1, 2)), jnp.tile(sin, (1, 2))


def _rot_columns(w):
    k, n = w.shape
    w4 = w.reshape(k, n // DA_HEAD_DIM, 4, DA_HEAD_DIM // 4)
    return jnp.stack([-w4[:, :, 1], w4[:, :, 0], -w4[:, :, 3], w4[:, :, 2]], axis=2).reshape(k, n)


def kernel(x, c, ctx, c_ctx, w_ada, b_ada, norm_pre_mix, norm_post_mix, norm_pre_ffn, norm_post_ffn, w_in,
           da_lambda_q1, da_lambda_k1, da_lambda_q2, da_lambda_k2, da_subln, hg_lower_bound, hg_norm, w_out,
           w_router, w_gate, w_up, w_down):
    bsz, t, d = x.shape
    ctx_len = ctx.shape[1]
    assert bsz <= 2 and d == 1024 and w_ada.shape[0] == 1
    cap = CAPACITY_FACTOR * t // N_EXPERTS
    lambda_init = 0.8 - 0.6 * math.exp(-0.3 * 0)

    lam = (jnp.exp(jnp.sum(da_lambda_q1[0].astype(F32) * da_lambda_k1[0].astype(F32)))
           - jnp.exp(jnp.sum(da_lambda_q2[0].astype(F32) * da_lambda_k2[0].astype(F32))) + lambda_init)
    lam = lam.reshape(1, 1)
    lb = jnp.cumsum(jax.nn.softmax(hg_lower_bound.astype(F32), axis=0), axis=0)[0]
    lb4 = lb.reshape(2, HG_HEADS, 1, HG_HEAD_DIM)

    wi = w_in[0]
    scale = DA_HEAD_DIM ** -0.5 * math.log2(math.e)
    wq = wi[:, 0:DA_WIDTH] * scale
    wk = wi[:, DA_WIDTH:2 * DA_WIDTH]
    w_all = jnp.concatenate([wq, wk, wi[:, 2 * DA_WIDTH:], _rot_columns(wq), _rot_columns(wk)],
                            axis=1).astype(BF16)
    cos, sin = _rope_tables(t, ctx_len)

    cmat = jnp.concatenate([c, c_ctx[None, :], jnp.zeros((8 - bsz - 1, d), F32)], axis=0)
    mod = _mod_call(cmat, w_ada[0], b_ada[0][None, :])

    qkv, z, iqg = _inproj_call(x, ctx, mod, norm_pre_mix[0][None, :], w_all, cos, sin)

    tq = min(ATTN_TQ, t)
    nk = t + ctx_len
    tk = ATTN_TK if nk % ATTN_TK == 0 else ctx_len
    out_da = _attn_call(lam, qkv, da_subln[0][None, :], t, tq, tk, 1.0 - lambda_init)
    o_fwd = _hgrn_call(z, iqg, lb4, ctx_len, 0)
    o_bwd = _hgrn_call(z, iqg, lb4, ctx_len, 1)

    wr = w_router[0].T
    wr_hi = wr.astype(BF16)
    wr_lo = (wr - wr_hi.astype(F32)).astype(BF16)
    x1, h2, aff = _post_call(out_da, o_fwd, o_bwd, iqg, x, mod, hg_norm[0][None, :], w_out[0].astype(BF16),
                             norm_post_mix[0][None, :], norm_pre_ffn[0][None, :], wr_hi, wr_lo, 256)
    idx, gate = _topk_call(aff, cap)
    f = _moe_call(idx.reshape(-1), h2, gate.reshape(bsz, N_EXPERTS, cap, 1), w_gate[0], w_up[0], w_down[0],
                  cap, 256)
    return _final_call(x1, f, mod, norm_post_ffn[0][None, :], 256)
```

```python
import functools
import math

import jax
import jax.numpy as jnp
import numpy as np
from jax import lax
from jax.experimental import pallas as pl
from jax.experimental.pallas import tpu as pltpu

F32 = jnp.float32
BF16 = jnp.bfloat16

GRID_W = 64
DA_HEADS = 4
DA_HEAD_DIM = 64
DA_V_DIM = 128
DA_WIDTH = 512
HG_HEADS = 4
HG_HEAD_DIM = 128
HG_WIDTH = 512
N_EXPERTS = 16
CAPACITY_FACTOR = 2
ROPE_THETA = 10000.0
NORM_EPS = 1e-6
LANES = 128
VMEM_LIMIT = 60 * 1024 * 1024
ATTN_TQ = 1024
ATTN_TK = 768
ATTN_ROW_BLOCK = 128
POST_TM = 512
SCATTER_GROUP = 8


def _cparams(sem, vmem=VMEM_LIMIT):
    return pltpu.CompilerParams(dimension_semantics=sem, vmem_limit_bytes=vmem)


def _dot(a, b):
    return jnp.dot(a, b, preferred_element_type=F32)


def _dot_nt(a, b):
    return lax.dot_general(a, b, (((1,), (1,)), ((), ())), preferred_element_type=F32)


def _dot_tn(a, b):
    return lax.dot_general(a, b, (((0,), (0,)), ((), ())), preferred_element_type=F32)


def _split3(x):
    hi = x.astype(BF16)
    r1 = x - hi.astype(F32)
    mid = r1.astype(BF16)
    lo = (r1 - mid.astype(F32)).astype(BF16)
    return hi, mid, lo


def _mod_kernel(s_ref, w_ref, b_ref, o_ref):
    s = s_ref[...]
    s = s * jax.nn.sigmoid(s)
    hi, mid, lo = _split3(s)
    w = w_ref[...]
    whi, wmid, wlo = _split3(w)
    acc = _dot(hi, whi) + (_dot(hi, wmid) + _dot(mid, whi)) + (_dot(hi, wlo) + _dot(mid, wmid) + _dot(lo, whi))
    o_ref[...] = acc + b_ref[...]


def _mod_call(cmat, w_ada, b_ada):
    d = cmat.shape[1]
    n = w_ada.shape[1]
    tn = 1536 if n % 1536 == 0 else n
    return pl.pallas_call(
        _mod_kernel,
        grid=(n // tn,),
        in_specs=[pl.BlockSpec((8, d), lambda j: (0, 0)),
                  pl.BlockSpec((d, tn), lambda j: (0, j)),
                  pl.BlockSpec((1, tn), lambda j: (0, j))],
        out_specs=pl.BlockSpec((8, tn), lambda j: (0, j)),
        out_shape=jax.ShapeDtypeStruct((8, n), F32),
        compiler_params=_cparams(("arbitrary",)),
        name="mod",
    )(cmat, w_ada, b_ada)


def _inproj_kernel(x_ref, ctx_ref, mod_ref, g_ref, w_ref, cos_ref, sin_ref, qkv_ref, z_ref, iqg_ref, *, n_lat, d):
    b = pl.program_id(0)
    i = pl.program_id(1)
    is_ctx = i == n_lat
    xb = jnp.where(is_ctx, ctx_ref[0], x_ref[0])
    row = jnp.where(is_ctx, 2, b)
    sh = mod_ref[pl.ds(row, 1), 0:d]
    sc = mod_ref[pl.ds(row, 1), d:2 * d]
    ms = jnp.mean(xb * xb, axis=-1, keepdims=True)
    h = xb * lax.rsqrt(ms + NORM_EPS) * g_ref[...]
    h = h * (1.0 + sc) + sh
    p = _dot(h.astype(BF16), w_ref[...])
    quarter = DA_HEAD_DIM // 4
    lane = lax.broadcasted_iota(jnp.int32, (p.shape[0], LANES), 1)
    first = (lane % (2 * quarter)) < quarter
    qk = []
    for j in range(2 * DA_HEADS):
        xj = p[:, j * LANES:(j + 1) * LANES]
        partner = jnp.where(first, pltpu.roll(xj, LANES - quarter, axis=1), pltpu.roll(xj, quarter, axis=1))
        qk.append(xj * cos_ref[...] + partner * sin_ref[...])
    qk = jnp.concatenate(qk, axis=-1)
    for j in range(2 * DA_HEADS):
        qkv_ref[0, j] = qk[:, j * LANES:(j + 1) * LANES].astype(BF16)
    for j in range(DA_HEADS):
        qkv_ref[0, 2 * DA_HEADS + j] = p[:, 1024 + j * LANES:1024 + (j + 1) * LANES].astype(BF16)
    z_ref[0] = p[:, 1536:2560]
    iqg_ref[0] = p[:, 2560:4096].astype(BF16)


def _inproj_call(x, ctx, mod, g, w, cos, sin):
    bsz, t, d = x.shape
    tm = ctx.shape[1]
    n_lat = t // tm
    rows = t + tm
    kern = functools.partial(_inproj_kernel, n_lat=n_lat, d=d)
    return pl.pallas_call(
        kern,
        grid=(bsz, n_lat + 1),
        in_specs=[pl.BlockSpec((1, tm, d), lambda b, i: (b, jnp.minimum(i, n_lat - 1), 0)),
                  pl.BlockSpec((1, tm, d), lambda b, i: (b, 0, 0)),
                  pl.BlockSpec(mod.shape, lambda b, i: (0, 0)),
                  pl.BlockSpec((1, d), lambda b, i: (0, 0)),
                  pl.BlockSpec(w.shape, lambda b, i: (0, 0)),
                  pl.BlockSpec((tm, LANES), lambda b, i: (i, 0)),
                  pl.BlockSpec((tm, LANES), lambda b, i: (i, 0))],
        out_specs=[pl.BlockSpec((1, 3 * DA_HEADS, tm, LANES), lambda b, i: (b, 0, i, 0)),
                   pl.BlockSpec((1, tm, 1024), lambda b, i: (b, i, 0)),
                   pl.BlockSpec((1, tm, 1536), lambda b, i: (b, i, 0))],
        out_shape=[jax.ShapeDtypeStruct((bsz, 3 * DA_HEADS, rows, LANES), BF16),
                   jax.ShapeDtypeStruct((bsz, rows, 1024), F32),
                   jax.ShapeDtypeStruct((bsz, rows, 1536), BF16)],
        compiler_params=_cparams(("parallel", "arbitrary")),
        name="inproj",
    )(x, ctx, mod, g, w, cos, sin)


def _attn_kernel(lam_ref, q_ref, k_ref, v_ref, subln_ref, o_ref, qs_ref, sa_ref, sb_ref, pa_ref, pb_ref, m_ref,
                 acc_ref, *, tq, tk, nchunk, out_scale):
    q = q_ref[0, 0]
    lane = lax.broadcasted_iota(jnp.int32, q.shape, 1)
    zero = jnp.zeros_like(q)
    qs_ref[0:tq, :] = jnp.where(lane < DA_HEAD_DIM, q, zero)
    qs_ref[tq:2 * tq, :] = jnp.where(lane >= DA_HEAD_DIM, q, zero)

    m_ref[...] = jnp.full(m_ref.shape, -1e30, F32)
    acc_ref[...] = jnp.zeros(acc_ref.shape, F32)

    def scores(j, s_ref):
        kj = k_ref[0, 0, pl.ds(pl.multiple_of(j * tk, tk), tk), :]
        s_ref[...] = _dot_nt(qs_ref[...], kj)

    def update(j, s_ref, p_ref):
        vj = v_ref[0, 0, pl.ds(pl.multiple_of(j * tk, tk), tk), :]
        vext = jnp.concatenate([vj, jnp.ones_like(vj)], axis=-1)
        m = m_ref[...]
        m_new = jnp.maximum(m, jnp.broadcast_to(jnp.max(s_ref[...], axis=-1, keepdims=True), m.shape))
        alpha = jnp.exp2(m - m_new)
        m_ref[...] = m_new
        rb = min(ATTN_ROW_BLOCK, 2 * tq)
        for r0 in range(0, 2 * tq, rb):
            rows = slice(r0, r0 + rb)
            mb = m_ref[rows, :]
            for c0 in range(0, tk, LANES):
                p_ref[rows, c0:c0 + LANES] = jnp.exp2((s_ref[rows, c0:c0 + LANES] - mb).astype(BF16))
        acc_ref[...] = jnp.tile(alpha, (1, 2)) * acc_ref[...] + _dot(p_ref[...], vext)

    npairs = (nchunk - 1) // 2
    scores(0, sa_ref)

    def body(i, _):
        scores(2 * i + 1, sb_ref)
        update(2 * i, sa_ref, pa_ref)
        scores(2 * i + 2, sa_ref)
        update(2 * i + 1, sb_ref, pb_ref)
        return 0
    lax.fori_loop(0, npairs, body, 0)
    if nchunk - 1 == 2 * npairs:
        update(nchunk - 1, sa_ref, pa_ref)
    else:
        scores(nchunk - 1, sb_ref)
        update(nchunk - 2, sa_ref, pa_ref)
        update(nchunk - 1, sb_ref, pb_ref)

    r = 1.0 / acc_ref[:, LANES:LANES + 1]
    acc = acc_ref[:, 0:LANES]
    o = acc[0:tq] * r[0:tq] - lam_ref[0, 0] * (acc[tq:2 * tq] * r[tq:2 * tq])
    ms = jnp.mean(o * o, axis=-1, keepdims=True)
    o = o * lax.rsqrt(ms + NORM_EPS) * subln_ref[...] * out_scale
    o_ref[0] = o.astype(o_ref.dtype)


def _attn_call(lam, qkv, subln, t, tq, tk, out_scale):
    bsz, _, rows, _ = qkv.shape
    kern = functools.partial(_attn_kernel, tq=tq, tk=tk, nchunk=rows // tk, out_scale=out_scale)
    return pl.pallas_call(
        kern,
        grid=(bsz, DA_HEADS, t // tq),
        in_specs=[pl.BlockSpec(memory_space=pltpu.SMEM),
                  pl.BlockSpec((1, 1, tq, LANES), lambda b, h, i: (b, h, i, 0)),
                  pl.BlockSpec((1, 1, rows, LANES), lambda b, h, i: (b, DA_HEADS + h, 0, 0)),
                  pl.BlockSpec((1, 1, rows, LANES), lambda b, h, i: (b, 2 * DA_HEADS + h, 0, 0)),
                  pl.BlockSpec((1, LANES), lambda b, h, i: (0, 0))],
        out_specs=pl.BlockSpec((1, tq, LANES), lambda b, h, i: (b, i, h)),
        out_shape=jax.ShapeDtypeStruct((bsz, t, DA_WIDTH), BF16),
        scratch_shapes=[pltpu.VMEM((2 * tq, LANES), BF16),
                        pltpu.VMEM((2 * tq, tk), F32),
                        pltpu.VMEM((2 * tq, tk), F32),
                        pltpu.VMEM((2 * tq, tk), BF16),
                        pltpu.VMEM((2 * tq, tk), BF16),
                        pltpu.VMEM((2 * tq, LANES), F32),
                        pltpu.VMEM((2 * tq, 2 * LANES), F32)],
        compiler_params=_cparams(("parallel", "parallel", "arbitrary")),
        name="attn",
    )(lam, qkv, qkv, qkv, subln)


HGRN_SUB_LEVELS = 3


def _hgrn_consts(c, d):
    nl = int(math.log2(c))
    t = np.arange(c)
    pi = t if d == 0 else c - 1 - t
    pr, pu = pi[:, None], pi[None, :]
    lm = np.zeros(((HGRN_SUB_LEVELS + 1) * c, c), np.float32)
    for lg in range(HGRN_SUB_LEVELS):
        same = (pr >> lg) == (pu >> lg)
        sec = ((pr >> lg) & 1) == 1
        lm[lg * c:(lg + 1) * c] = same & np.where(sec, pu <= pr, pu > pr)
    lm[HGRN_SUB_LEVELS * c:] = pu <= pr
    masks = np.zeros((nl + 1, c, c), np.float32)
    for lg in range(nl):
        masks[lg] = (t[:, None] >> (lg + 1)) == (t[None, :] >> (lg + 1))
    masks[nl] = np.eye(c)
    return jnp.asarray(lm, BF16), jnp.asarray(masks, F32), nl


def _hgrn_kernel(z_ref, i_ref, q_ref, lb_ref, lmat_ref, mask_ref, o_ref, st_ref, *, c, nl, d):
    s = pl.program_id(1)
    heads = range(HG_HEADS)

    @pl.when(s == 0)
    def _():
        st_ref[...] = jnp.zeros(st_ref.shape, F32)

    def col(h):
        return slice(h * LANES, (h + 1) * LANES)

    f = [lb_ref[0, h] + (1.0 - lb_ref[0, h]) * jax.nn.sigmoid(z_ref[0, :, col(h)]) for h in heads]
    lf = [jnp.log(f[h]) for h in heads]
    kk = [1.0 - f[h] for h in heads]
    v = [i_ref[0, :, col(h)] for h in heads]
    q = [q_ref[0, :, col(h)].astype(F32) for h in heads]
    x = []
    for h in heads:
        hi = lf[h].astype(BF16)
        lo = (lf[h] - hi.astype(F32)).astype(BF16)
        x2 = _dot(lmat_ref[...], jnp.concatenate([hi, lo], axis=-1))
        x.append(x2[:, 0:LANES] + x2[:, LANES:2 * LANES])
    cum = [x[h][HGRN_SUB_LEVELS * c:] for h in heads]
    last = c - 1 if d == 0 else 0
    bend = [cum[h][last:last + 1] for h in heads]

    row = lax.broadcasted_iota(jnp.int32, (c, LANES), 0)
    scores = [mask_ref[nl] * _dot_nt(q[h].astype(BF16), kk[h].astype(BF16)) for h in heads]
    for lg in range(nl):
        hs = 1 << lg
        sec = ((row >> lg) & 1) == (1 - d)
        for h in heads:
            if lg < HGRN_SUB_LEVELS:
                xl = x[h][lg * c:(lg + 1) * c]
            else:
                mid = hs - 1 if d == 0 else hs
                c3 = cum[h].reshape(c // (2 * hs), 2 * hs, LANES)
                ref = jnp.broadcast_to(c3[:, mid:mid + 1, :], c3.shape).reshape(c, LANES)
                xl = -jnp.abs(cum[h] - ref)
            g = jnp.exp(xl)
            qt = jnp.where(sec, q[h] * g, 0.0).astype(BF16)
            kt = jnp.where(sec, 0.0, kk[h] * g).astype(BF16)
            scores[h] = scores[h] + mask_ref[lg] * _dot_nt(qt, kt)
    for h in heads:
        st = st_ref[h]
        o = (_dot_nt((q[h] * jnp.exp(cum[h])).astype(BF16), st.astype(BF16))
             + _dot(scores[h].astype(BF16), v[h]))
        kb = (kk[h] * jnp.exp(bend[h] - cum[h])).astype(BF16)
        st_ref[h] = st * jnp.exp(bend[h]) + _dot_tn(v[h], kb)
        o_ref[0, :, col(h)] = o.astype(o_ref.dtype)


def _hgrn_call(z, iqg, lb4, c, d):
    bsz, rows, _ = z.shape
    nb = rows // c - 1
    lmat, masks, nl = _hgrn_consts(c, d)

    def blk(s):
        return jnp.where(s == 0, nb, s - 1 if d == 0 else nb - s)

    kern = functools.partial(_hgrn_kernel, c=c, nl=nl, d=d)
    return pl.pallas_call(
        kern,
        grid=(bsz, nb + 1),
        in_specs=[pl.BlockSpec((1, c, HG_WIDTH), lambda b, s: (b, blk(s), d)),
                  pl.BlockSpec((1, c, HG_WIDTH), lambda b, s: (b, blk(s), 0)),
                  pl.BlockSpec((1, c, HG_WIDTH), lambda b, s: (b, blk(s), 1)),
                  pl.BlockSpec((1, HG_HEADS, 1, LANES), lambda b, s: (d, 0, 0, 0)),
                  pl.BlockSpec(lmat.shape, lambda b, s: (0, 0)),
                  pl.BlockSpec(masks.shape, lambda b, s: (0, 0, 0))],
        out_specs=pl.BlockSpec((1, c, HG_WIDTH), lambda b, s: (b, blk(s), 0)),
        out_shape=jax.ShapeDtypeStruct((bsz, rows, HG_WIDTH), BF16),
        scratch_shapes=[pltpu.VMEM((HG_HEADS, HG_HEAD_DIM, HG_HEAD_DIM), F32)],
        compiler_params=_cparams(("parallel", "arbitrary")),
        name="hgrn_bwd" if d else "hgrn_fwd",
    )(z, iqg, iqg, lb4, lmat, masks)


def _post_kernel(da_ref, of_ref, ob_ref, g_ref, x_ref, mod_ref, hgn_ref, wout_ref, npost_ref, npre_ref,
                 wrh_ref, wrl_ref, x1_ref, h2_ref, aff_ref, *, d):
    b = pl.program_id(0)
    o = of_ref[0].astype(F32) + ob_ref[0].astype(F32)
    parts = []
    for h in range(HG_HEADS):
        oh = o[:, h * LANES:(h + 1) * LANES]
        ms = jnp.mean(oh * oh, axis=-1, keepdims=True)
        parts.append(oh * lax.rsqrt(ms + NORM_EPS) * hgn_ref[...])
    g = g_ref[0].astype(F32)
    ohn = jnp.concatenate(parts, axis=-1) * (g * jax.nn.sigmoid(g))
    mixed = jnp.concatenate([da_ref[0], ohn.astype(BF16)], axis=-1)
    y = _dot(mixed, wout_ref[...])
    yn = y * lax.rsqrt(jnp.mean(y * y, axis=-1, keepdims=True) + NORM_EPS) * npost_ref[...]
    gt1 = mod_ref[pl.ds(b, 1), 2 * d:3 * d]
    sh2 = mod_ref[pl.ds(b, 1), 3 * d:4 * d]
    sc2 = mod_ref[pl.ds(b, 1), 4 * d:5 * d]
    x1 = x_ref[0] + gt1 * yn
    x1_ref[0] = x1
    h2 = x1 * lax.rsqrt(jnp.mean(x1 * x1, axis=-1, keepdims=True) + NORM_EPS) * npre_ref[...]
    h2 = h2 * (1.0 + sc2) + sh2
    h2_ref[0] = h2
    hh = h2.astype(BF16)
    hl = (h2 - hh.astype(F32)).astype(BF16)
    logits = _dot_nt(wrh_ref[...], hh) + (_dot_nt(wrh_ref[...], hl) + _dot_nt(wrl_ref[...], hh))
    mx = jnp.max(logits, axis=0, keepdims=True)
    e = jnp.exp(logits - mx)
    aff_ref[0] = e / jnp.sum(e, axis=0, keepdims=True)


def _post_call(out_da, o_fwd, o_bwd, iqg, x, mod, hg_norm, w_out, npost, npre, wr_hi, wr_lo, tm):
    bsz, t, d = x.shape
    kern = functools.partial(_post_kernel, d=d)
    row = lambda b, i: (0, 0)
    return pl.pallas_call(
        kern,
        grid=(bsz, t // tm),
        in_specs=[pl.BlockSpec((1, tm, DA_WIDTH), lambda b, i: (b, i, 0)),
                  pl.BlockSpec((1, tm, HG_WIDTH), lambda b, i: (b, i, 0)),
                  pl.BlockSpec((1, tm, HG_WIDTH), lambda b, i: (b, i, 0)),
                  pl.BlockSpec((1, tm, HG_WIDTH), lambda b, i: (b, i, 2)),
                  pl.BlockSpec((1, tm, d), lambda b, i: (b, i, 0)),
                  pl.BlockSpec(mod.shape, row),
                  pl.BlockSpec((1, LANES), row),
                  pl.BlockSpec(w_out.shape, row),
                  pl.BlockSpec((1, d), row),
                  pl.BlockSpec((1, d), row),
                  pl.BlockSpec(wr_hi.shape, row),
                  pl.BlockSpec(wr_lo.shape, row)],
        out_specs=[pl.BlockSpec((1, tm, d), lambda b, i: (b, i, 0)),
                   pl.BlockSpec((1, tm, d), lambda b, i: (b, i, 0)),
                   pl.BlockSpec((1, N_EXPERTS, tm), lambda b, i: (b, 0, i))],
        out_shape=[jax.ShapeDtypeStruct((bsz, t, d), F32),
                   jax.ShapeDtypeStruct((bsz, t, d), F32),
                   jax.ShapeDtypeStruct((bsz, N_EXPERTS, t), F32)],
        compiler_params=_cparams(("parallel", "arbitrary")),
        name="post",
    )(out_da, o_fwd, o_bwd, iqg, x, mod, hg_norm, w_out, npost, npre, wr_hi, wr_lo)


TOPK_BLOCK = 256
TOPK_COUNT_BLOCK = 1024


def _topk_kernel(aff_ref, utri_ref, idx_ref, gate_ref, selc_ref, cnt_ref, cnt_smem, acci_ref, accg_ref,
                 *, t, cap):
    ne = N_EXPERTS
    cb = TOPK_BLOCK
    nblk = t // cb
    kb = min(TOPK_COUNT_BLOCK, t)

    def as_float(bits):
        return lax.bitcast_convert_type(bits, F32)

    def count_ge(thr):
        thr_b = jnp.broadcast_to(thr, (ne, kb))

        def body(i, acc):
            a = aff_ref[0, :, pl.ds(pl.multiple_of(i * kb, kb), kb)]
            return acc + jnp.where(a >= thr_b, 1, 0)
        acc = lax.fori_loop(0, t // kb, body, jnp.zeros((ne, kb), jnp.int32))
        return jnp.sum(acc, axis=1, keepdims=True)

    def bit_body(k, tau):
        cand = tau | (jnp.int32(1) << (29 - k))
        return jnp.where(count_ge(as_float(cand)) >= cap, cand, tau)
    tau = lax.fori_loop(0, 30, bit_body, jnp.zeros((ne, 1), jnp.int32))
    tau_f = as_float(tau)
    nxt_f = as_float(tau + 1)

    n_gt = count_ge(nxt_f)
    need = (cap - n_gt).astype(F32)

    utri = utri_ref[...]

    def sel_body(i, carry):
        ceq, csel = carry
        a = aff_ref[0, :, pl.ds(pl.multiple_of(i * cb, cb), cb)]
        gt = a >= nxt_f
        eq = (a >= tau_f) & jnp.logical_not(gt)
        eq_rank = _dot(jnp.where(eq, 1.0, 0.0).astype(BF16), utri) + ceq
        sel = gt | (eq & (eq_rank <= need))
        c_incl = _dot(jnp.where(sel, 1.0, 0.0).astype(BF16), utri) + csel
        selc_ref[:, pl.ds(pl.multiple_of(i * cb, cb), cb)] = jnp.where(sel, c_incl, 0.0)
        cnt_ref[i] = jnp.broadcast_to(csel, (ne, LANES)).astype(jnp.int32)
        return eq_rank[:, cb - 1:cb], c_incl[:, cb - 1:cb]
    zero = jnp.zeros((ne, 1), F32)
    _, c_all = lax.fori_loop(0, nblk, sel_body, (zero, zero))
    cnt_ref[nblk] = jnp.broadcast_to(c_all, (ne, LANES)).astype(jnp.int32)

    def to_smem(i, _):
        blk = cnt_ref[i]
        for e in range(ne):
            cnt_smem[i * ne + e] = blk[e, 0]
        return 0
    lax.fori_loop(0, nblk + 1, to_smem, 0)

    n_pt = cap // LANES
    slot = lax.broadcasted_iota(jnp.int32, (LANES, cb), 0)
    for e in range(ne):
        acci_ref[...] = jnp.zeros(acci_ref.shape, F32)
        accg_ref[...] = jnp.zeros(accg_ref.shape, F32)

        def slot_body(i, _):
            cols = pl.ds(pl.multiple_of(i * cb, cb), cb)
            cc = jnp.broadcast_to(selc_ref[e:e + 1, cols], (LANES, cb))
            aa = jnp.broadcast_to(aff_ref[0, e:e + 1, cols], (LANES, cb))
            tt = (lax.broadcasted_iota(jnp.int32, (LANES, cb), 1) + i * cb).astype(F32)
            first = cnt_smem[i * ne + e]
            last = cnt_smem[(i + 1) * ne + e] - 1

            def tile_body(j, _):
                hit = cc == (slot + (j * LANES + 1)).astype(F32)
                ti = jnp.where(hit, tt, 0.0)
                gi = jnp.where(hit, aa, 0.0)
                for c0 in range(0, cb, LANES):
                    acci_ref[j] = acci_ref[j] + ti[:, c0:c0 + LANES]
                    accg_ref[j] = accg_ref[j] + gi[:, c0:c0 + LANES]
                return 0
            lax.fori_loop(first // LANES, last // LANES + 1, tile_body, 0)
            return 0
        lax.fori_loop(0, nblk, slot_body, 0)
        for j in range(n_pt):
            idx_ref[0, e:e + 1, j * LANES:(j + 1) * LANES] = (
                jnp.sum(acci_ref[j].T, axis=0, keepdims=True).astype(jnp.int32))
            gate_ref[0, e:e + 1, j * LANES:(j + 1) * LANES] = jnp.sum(accg_ref[j].T, axis=0, keepdims=True)


def _topk_call(aff, cap):
    bsz, ne, t = aff.shape
    cb = TOPK_BLOCK
    utri = jnp.asarray(np.triu(np.ones((cb, cb), np.float32)), BF16)
    kern = functools.partial(_topk_kernel, t=t, cap=cap)
    return pl.pallas_call(
        kern,
        grid=(bsz,),
        in_specs=[pl.BlockSpec((1, ne, t), lambda b: (b, 0, 0)),
                  pl.BlockSpec((cb, cb), lambda b: (0, 0))],
        out_specs=[pl.BlockSpec((1, ne, cap), lambda b: (b, 0, 0)),
                   pl.BlockSpec((1, ne, cap), lambda b: (b, 0, 0))],
        out_shape=[jax.ShapeDtypeStruct((bsz, ne, cap), jnp.int32),
                   jax.ShapeDtypeStruct((bsz, ne, cap), F32)],
        scratch_shapes=[pltpu.VMEM((ne, t), F32),
                        pltpu.VMEM((t // cb + 1, ne, LANES), jnp.int32),
                        pltpu.SMEM(((t // cb + 1) * ne,), jnp.int32),
                        pltpu.VMEM((cap // LANES, LANES, LANES), F32),
                        pltpu.VMEM((cap // LANES, LANES, LANES), F32)],
        compiler_params=_cparams(("arbitrary",)),
        name="topk",
    )(aff, utri)


def _moe_kernel(idx_ref, h2_ref, gate_ref, wg_ref, wu_ref, wd_ref, f_ref,
                f_scr, xin, xbf, yacc, gsem, osem, *, t, cap, ne, nf, zr):
    b = pl.program_id(0)
    e = pl.program_id(1)
    fc = pl.program_id(2)
    nb = pl.num_programs(0)
    flat = b * ne + e
    base = flat * cap
    slot = flat % 2
    last = nb * ne - 1
    nxt = jnp.minimum(flat + 1, last)
    rows_per_fc = cap // nf

    @pl.when((e == 0) & (fc == 0))
    def _():
        def zbody(i, _):
            f_scr[pl.ds(i * zr, zr), :] = jnp.zeros((zr, f_scr.shape[1]), F32)
            return 0
        lax.fori_loop(0, t // zr, zbody, 0)

    def row_copy(step, dst, p):
        tok = idx_ref[step * cap + p]
        return pltpu.make_async_copy(h2_ref.at[step // ne, pl.ds(tok, 1), :],
                                     xin.at[dst, pl.ds(p, 1), :], gsem.at[dst])

    def wait_rows(dst):
        pltpu.make_async_copy(h2_ref.at[0, pl.ds(0, cap), :], xin.at[dst], gsem.at[dst]).wait()

    @pl.when((flat == 0) & (fc == 0))
    def _():
        def gstart(p, _):
            row_copy(flat, slot, p).start()
            return 0
        lax.fori_loop(0, cap, gstart, 0, unroll=8)

    @pl.when(fc == 0)
    def _():
        wait_rows(slot)
        xbf[...] = xin[slot].astype(BF16)

    for p in range(rows_per_fc):
        row_copy(nxt, 1 - slot, fc * rows_per_fc + p).start()

    x = xbf[...]
    g = _dot(x, wg_ref[0].astype(BF16))
    u = _dot(x, wu_ref[0].astype(BF16))
    hmid = (g * jax.nn.sigmoid(g) * u).astype(BF16)
    contrib = _dot(hmid, wd_ref[0].astype(BF16))

    @pl.when(fc == 0)
    def _():
        yacc[...] = contrib

    @pl.when(fc != 0)
    def _():
        yacc[...] = yacc[...] + contrib

    @pl.when(fc == nf - 1)
    def _():
        yacc[...] = yacc[...] * gate_ref[0, 0]

        def sbody(g, _):
            p0 = g * SCATTER_GROUP
            toks = [idx_ref[base + p0 + i] for i in range(SCATTER_GROUP)]
            yr = [yacc[pl.ds(p0 + i, 1), :] for i in range(SCATTER_GROUP)]
            fr = [f_scr[pl.ds(toks[i], 1), :] for i in range(SCATTER_GROUP)]
            for i in range(SCATTER_GROUP):
                f_scr[pl.ds(toks[i], 1), :] = fr[i] + yr[i]
            return 0
        lax.fori_loop(0, cap // SCATTER_GROUP, sbody, 0)

    @pl.when((e == ne - 1) & (fc == nf - 1))
    def _():
        cp = pltpu.make_async_copy(f_scr, f_ref.at[b], osem)
        cp.start()
        cp.wait()

    @pl.when((flat == last) & (fc == nf - 1))
    def _():
        wait_rows(1 - slot)


def _moe_call(idx_flat, h2, gate4, w_gate, w_up, w_down, cap, tf):
    bsz, t, d = h2.shape
    ne, _, ff = w_gate.shape
    nf = ff // tf
    kern = functools.partial(_moe_kernel, t=t, cap=cap, ne=ne, nf=nf, zr=256)
    grid_spec = pltpu.PrefetchScalarGridSpec(
        num_scalar_prefetch=1,
        grid=(bsz, ne, nf),
        in_specs=[pl.BlockSpec(memory_space=pl.ANY),
                  pl.BlockSpec((1, 1, cap, 1), lambda b, e, f, idx: (b, e, 0, 0)),
                  pl.BlockSpec((1, d, tf), lambda b, e, f, idx: (e, 0, f)),
                  pl.BlockSpec((1, d, tf), lambda b, e, f, idx: (e, 0, f)),
                  pl.BlockSpec((1, tf, d), lambda b, e, f, idx: (e, f, 0))],
        out_specs=pl.BlockSpec(memory_space=pl.ANY),
        scratch_shapes=[pltpu.VMEM((t, d), F32),
                        pltpu.VMEM((2, cap, d), F32),
                        pltpu.VMEM((cap, d), BF16),
                        pltpu.VMEM((cap, d), F32),
                        pltpu.SemaphoreType.DMA((2,)),
                        pltpu.SemaphoreType.DMA(())],
    )
    return pl.pallas_call(
        kern,
        grid_spec=grid_spec,
        out_shape=jax.ShapeDtypeStruct((bsz, t, d), F32),
        compiler_params=_cparams(("arbitrary", "arbitrary", "arbitrary")),
        name="moe",
    )(idx_flat, h2, gate4, w_gate, w_up, w_down)


def _final_kernel(x1_ref, f_ref, mod_ref, n_ref, o_ref, *, d):
    b = pl.program_id(0)
    f = f_ref[0]
    fn = f * lax.rsqrt(jnp.mean(f * f, axis=-1, keepdims=True) + NORM_EPS) * n_ref[...]
    gt2 = mod_ref[pl.ds(b, 1), 5 * d:6 * d]
    o_ref[0] = x1_ref[0] + gt2 * fn


def _final_call(x1, f, mod, npost, tm):
    bsz, t, d = x1.shape
    kern = functools.partial(_final_kernel, d=d)
    return pl.pallas_call(
        kern,
        grid=(bsz, t // tm),
        in_specs=[pl.BlockSpec((1, tm, d), lambda b, i: (b, i, 0)),
                  pl.BlockSpec((1, tm, d), lambda b, i: (b, i, 0)),
                  pl.BlockSpec(mod.shape, lambda b, i: (0, 0)),
                  pl.BlockSpec((1, d), lambda b, i: (0, 0))],
        out_specs=pl.BlockSpec((1, tm, d), lambda b, i: (b, i, 0)),
        out_shape=jax.ShapeDtypeStruct((bsz, t, d), F32),
        compiler_params=_cparams(("parallel", "arbitrary")),
        name="final",
    )(x1, f, mod, npost)


def _rope_tables(t, ctx_len):
    half = DA_HEAD_DIM // 2
    inv_freq = 1.0 / (ROPE_THETA ** (jnp.arange(0, half, 2, dtype=F32) / half))
    pos = jnp.arange(t, dtype=jnp.int32)
    r = (pos // GRID_W).astype(F32)
    col = (pos % GRID_W).astype(F32)
    ang_r = r[:, None] * inv_freq
    ang_c = col[:, None] * inv_freq
    ang = jnp.concatenate([ang_r, ang_r, ang_c, ang_c], axis=-1)
    cos = jnp.concatenate([jnp.cos(ang), jnp.ones((ctx_len, DA_HEAD_DIM), F32)], axis=0)
    sin = jnp.concatenate([jnp.sin(ang), jnp.zeros((ctx_len, DA_HEAD_DIM), F32)], axis=0)
    quarter = DA_HEAD_DIM // 4
    sign = jnp.where((jnp.arange(DA_HEAD_DIM) % (2 * quarter)) < quarter, -1.0, 1.0).astype(F32)
    return jnp.tile(cos, (1, 2)), jnp.tile(sin * sign, (1, 2))


def kernel(x, c, ctx, c_ctx, w_ada, b_ada, norm_pre_mix, norm_post_mix, norm_pre_ffn, norm_post_ffn, w_in,
           da_lambda_q1, da_lambda_k1, da_lambda_q2, da_lambda_k2, da_subln, hg_lower_bound, hg_norm, w_out,
           w_router, w_gate, w_up, w_down):
    bsz, t, d = x.shape
    ctx_len = ctx.shape[1]
    assert bsz <= 2 and d == 1024 and w_ada.shape[0] == 1
    cap = CAPACITY_FACTOR * t // N_EXPERTS
    lambda_init = 0.8 - 0.6 * math.exp(-0.3 * 0)

    lam = (jnp.exp(jnp.sum(da_lambda_q1[0].astype(F32) * da_lambda_k1[0].astype(F32)))
           - jnp.exp(jnp.sum(da_lambda_q2[0].astype(F32) * da_lambda_k2[0].astype(F32))) + lambda_init)
    lam = lam.reshape(1, 1)
    lb = jnp.cumsum(jax.nn.softmax(hg_lower_bound.astype(F32), axis=0), axis=0)[0]
    lb4 = lb.reshape(2, HG_HEADS, 1, HG_HEAD_DIM)

    wi = w_in[0]
    scale = DA_HEAD_DIM ** -0.5 * math.log2(math.e)
    wq = wi[:, 0:DA_WIDTH] * scale
    w_all = jnp.concatenate([wq, wi[:, DA_WIDTH:]], axis=1).astype(BF16)
    cos, sin = _rope_tables(t, ctx_len)

    cmat = jnp.concatenate([c, c_ctx[None, :], jnp.zeros((8 - bsz - 1, d), F32)], axis=0)
    mod = _mod_call(cmat, w_ada[0], b_ada[0][None, :])

    qkv, z, iqg = _inproj_call(x, ctx, mod, norm_pre_mix[0][None, :], w_all, cos, sin)

    tq = min(ATTN_TQ, t)
    nk = t + ctx_len
    tk = ATTN_TK if nk % ATTN_TK == 0 else ctx_len
    out_da = _attn_call(lam, qkv, da_subln[0][None, :], t, tq, tk, 1.0 - lambda_init)
    o_fwd = _hgrn_call(z, iqg, lb4, ctx_len, 0)
    o_bwd = _hgrn_call(z, iqg, lb4, ctx_len, 1)

    wr = w_router[0].T
    wr_hi = wr.astype(BF16)
    wr_lo = (wr - wr_hi.astype(F32)).astype(BF16)
    x1, h2, aff = _post_call(out_da, o_fwd, o_bwd, iqg, x, mod, hg_norm[0][None, :], w_out[0].astype(BF16),
                             norm_post_mix[0][None, :], norm_pre_ffn[0][None, :], wr_hi, wr_lo, POST_TM)
    idx, gate = _topk_call(aff, cap)
    f = _moe_call(idx.reshape(-1), h2, gate.reshape(bsz, N_EXPERTS, cap, 1), w_gate[0], w_up[0], w_down[0],
                  cap, 256)
    return _final_call(x1, f, mod, norm_post_ffn[0][None, :], 256)
```

```python
import functools
import math

import jax
import jax.numpy as jnp
import numpy as np
from jax import lax
from jax.experimental import pallas as pl
from jax.experimental.pallas import tpu as pltpu

F32 = jnp.float32
BF16 = jnp.bfloat16

GRID_W = 64
DA_HEADS = 4
DA_HEAD_DIM = 64
DA_V_DIM = 128
DA_WIDTH = 512
HG_HEADS = 4
HG_HEAD_DIM = 128
HG_WIDTH = 512
N_EXPERTS = 16
CAPACITY_FACTOR = 2
ROPE_THETA = 10000.0
NORM_EPS = 1e-6
LANES = 128
VMEM_LIMIT = 60 * 1024 * 1024
ATTN_TQ = 1024
ATTN_TK = 768
ATTN_ROW_BLOCK = 128
POST_TM = 512
SCATTER_GROUP = 8


def _cparams(sem, vmem=VMEM_LIMIT):
    return pltpu.CompilerParams(dimension_semantics=sem, vmem_limit_bytes=vmem)


def _dot(a, b):
    return jnp.dot(a, b, preferred_element_type=F32)


def _dot_nt(a, b):
    return lax.dot_general(a, b, (((1,), (1,)), ((), ())), preferred_element_type=F32)


def _dot_tn(a, b):
    return lax.dot_general(a, b, (((0,), (0,)), ((), ())), preferred_element_type=F32)


def _split3(x):
    hi = x.astype(BF16)
    r1 = x - hi.astype(F32)
    mid = r1.astype(BF16)
    lo = (r1 - mid.astype(F32)).astype(BF16)
    return hi, mid, lo


def _mod_kernel(s_ref, w_ref, b_ref, o_ref):
    s = s_ref[...]
    s = s * jax.nn.sigmoid(s)
    hi, mid, lo = _split3(s)
    w = w_ref[...]
    whi, wmid, wlo = _split3(w)
    acc = _dot(hi, whi) + (_dot(hi, wmid) + _dot(mid, whi)) + (_dot(hi, wlo) + _dot(mid, wmid) + _dot(lo, whi))
    o_ref[...] = acc + b_ref[...]


def _mod_call(cmat, w_ada, b_ada):
    d = cmat.shape[1]
    n = w_ada.shape[1]
    tn = 1536 if n % 1536 == 0 else n
    return pl.pallas_call(
        _mod_kernel,
        grid=(n // tn,),
        in_specs=[pl.BlockSpec((8, d), lambda j: (0, 0)),
                  pl.BlockSpec((d, tn), lambda j: (0, j)),
                  pl.BlockSpec((1, tn), lambda j: (0, j))],
        out_specs=pl.BlockSpec((8, tn), lambda j: (0, j)),
        out_shape=jax.ShapeDtypeStruct((8, n), F32),
        compiler_params=_cparams(("arbitrary",)),
        name="mod",
    )(cmat, w_ada, b_ada)


def _inproj_kernel(x_ref, ctx_ref, mod_ref, g_ref, w_ref, cos_ref, sin_ref, qkv_ref, z_ref, iqg_ref, *, n_lat, d):
    b = pl.program_id(0)
    i = pl.program_id(1)
    is_ctx = i == n_lat
    xb = jnp.where(is_ctx, ctx_ref[0], x_ref[0])
    row = jnp.where(is_ctx, 2, b)
    sh = mod_ref[pl.ds(row, 1), 0:d]
    sc = mod_ref[pl.ds(row, 1), d:2 * d]
    ms = jnp.mean(xb * xb, axis=-1, keepdims=True)
    h = xb * lax.rsqrt(ms + NORM_EPS) * g_ref[...]
    h = h * (1.0 + sc) + sh
    p = _dot(h.astype(BF16), w_ref[...])
    quarter = DA_HEAD_DIM // 4
    lane = lax.broadcasted_iota(jnp.int32, (p.shape[0], LANES), 1)
    first = (lane % (2 * quarter)) < quarter
    qk = []
    for j in range(2 * DA_HEADS):
        xj = p[:, j * LANES:(j + 1) * LANES]
        partner = jnp.where(first, pltpu.roll(xj, LANES - quarter, axis=1), pltpu.roll(xj, quarter, axis=1))
        qk.append(xj * cos_ref[...] + partner * sin_ref[...])
    qk = jnp.concatenate(qk, axis=-1)
    for j in range(2 * DA_HEADS):
        qkv_ref[0, j] = qk[:, j * LANES:(j + 1) * LANES].astype(BF16)
    for j in range(DA_HEADS):
        qkv_ref[0, 2 * DA_HEADS + j] = p[:, 1024 + j * LANES:1024 + (j + 1) * LANES].astype(BF16)
    z_ref[0] = p[:, 1536:2560]
    iqg_ref[0] = p[:, 2560:4096].astype(BF16)


def _inproj_call(x, ctx, mod, g, w, cos, sin):
    bsz, t, d = x.shape
    tm = ctx.shape[1]
    n_lat = t // tm
    rows = t + tm
    kern = functools.partial(_inproj_kernel, n_lat=n_lat, d=d)
    return pl.pallas_call(
        kern,
        grid=(bsz, n_lat + 1),
        in_specs=[pl.BlockSpec((1, tm, d), lambda b, i: (b, jnp.minimum(i, n_lat - 1), 0)),
                  pl.BlockSpec((1, tm, d), lambda b, i: (b, 0, 0)),
                  pl.BlockSpec(mod.shape, lambda b, i: (0, 0)),
                  pl.BlockSpec((1, d), lambda b, i: (0, 0)),
                  pl.BlockSpec(w.shape, lambda b, i: (0, 0)),
                  pl.BlockSpec((tm, LANES), lambda b, i: (i, 0)),
                  pl.BlockSpec((tm, LANES), lambda b, i: (i, 0))],
        out_specs=[pl.BlockSpec((1, 3 * DA_HEADS, tm, LANES), lambda b, i: (b, 0, i, 0)),
                   pl.BlockSpec((1, tm, 1024), lambda b, i: (b, i, 0)),
                   pl.BlockSpec((1, tm, 1536), lambda b, i: (b, i, 0))],
        out_shape=[jax.ShapeDtypeStruct((bsz, 3 * DA_HEADS, rows, LANES), BF16),
                   jax.ShapeDtypeStruct((bsz, rows, 1024), F32),
                   jax.ShapeDtypeStruct((bsz, rows, 1536), BF16)],
        compiler_params=_cparams(("parallel", "arbitrary")),
        name="inproj",
    )(x, ctx, mod, g, w, cos, sin)


def _attn_kernel(lam_ref, q_ref, k_ref, v_ref, subln_ref, o_ref, qs_ref, sa_ref, sb_ref, pa_ref, pb_ref, m_ref,
                 acc_ref, *, tq, tk, nchunk, out_scale):
    q = q_ref[0, 0]
    lane = lax.broadcasted_iota(jnp.int32, q.shape, 1)
    zero = jnp.zeros_like(q)
    qs_ref[0:tq, :] = jnp.where(lane < DA_HEAD_DIM, q, zero)
    qs_ref[tq:2 * tq, :] = jnp.where(lane >= DA_HEAD_DIM, q, zero)

    m_ref[...] = jnp.full(m_ref.shape, -1e30, F32)
    acc_ref[...] = jnp.zeros(acc_ref.shape, F32)

    def scores(j, s_ref):
        kj = k_ref[0, 0, pl.ds(pl.multiple_of(j * tk, tk), tk), :]
        s_ref[...] = _dot_nt(qs_ref[...], kj)

    def update(j, s_ref, p_ref):
        vj = v_ref[0, 0, pl.ds(pl.multiple_of(j * tk, tk), tk), :]
        vext = jnp.concatenate([vj, jnp.ones_like(vj)], axis=-1)
        m = m_ref[...]
        m_new = jnp.maximum(m, jnp.broadcast_to(jnp.max(s_ref[...], axis=-1, keepdims=True), m.shape))
        alpha = jnp.exp2(m - m_new)
        m_ref[...] = m_new
        rb = min(ATTN_ROW_BLOCK, 2 * tq)
        for r0 in range(0, 2 * tq, rb):
            rows = slice(r0, r0 + rb)
            mb = m_ref[rows, :]
            for c0 in range(0, tk, LANES):
                p_ref[rows, c0:c0 + LANES] = jnp.exp2((s_ref[rows, c0:c0 + LANES] - mb).astype(BF16))
        acc_ref[...] = jnp.tile(alpha, (1, 2)) * acc_ref[...] + _dot(p_ref[...], vext)

    npairs = (nchunk - 1) // 2
    scores(0, sa_ref)

    def body(i, _):
        scores(2 * i + 1, sb_ref)
        update(2 * i, sa_ref, pa_ref)
        scores(2 * i + 2, sa_ref)
        update(2 * i + 1, sb_ref, pb_ref)
        return 0
    lax.fori_loop(0, npairs, body, 0)
    if nchunk - 1 == 2 * npairs:
        update(nchunk - 1, sa_ref, pa_ref)
    else:
        scores(nchunk - 1, sb_ref)
        update(nchunk - 2, sa_ref, pa_ref)
        update(nchunk - 1, sb_ref, pb_ref)

    r = 1.0 / acc_ref[:, LANES:LANES + 1]
    acc = acc_ref[:, 0:LANES]
    o = acc[0:tq] * r[0:tq] - lam_ref[0, 0] * (acc[tq:2 * tq] * r[tq:2 * tq])
    ms = jnp.mean(o * o, axis=-1, keepdims=True)
    o = o * lax.rsqrt(ms + NORM_EPS) * subln_ref[...] * out_scale
    o_ref[0] = o.astype(o_ref.dtype)


def _attn_call(lam, qkv, subln, t, tq, tk, out_scale):
    bsz, _, rows, _ = qkv.shape
    kern = functools.partial(_attn_kernel, tq=tq, tk=tk, nchunk=rows // tk, out_scale=out_scale)
    return pl.pallas_call(
        kern,
        grid=(bsz, DA_HEADS, t // tq),
        in_specs=[pl.BlockSpec(memory_space=pltpu.SMEM),
                  pl.BlockSpec((1, 1, tq, LANES), lambda b, h, i: (b, h, i, 0)),
                  pl.BlockSpec((1, 1, rows, LANES), lambda b, h, i: (b, DA_HEADS + h, 0, 0)),
                  pl.BlockSpec((1, 1, rows, LANES), lambda b, h, i: (b, 2 * DA_HEADS + h, 0, 0)),
                  pl.BlockSpec((1, LANES), lambda b, h, i: (0, 0))],
        out_specs=pl.BlockSpec((1, tq, LANES), lambda b, h, i: (b, i, h)),
        out_shape=jax.ShapeDtypeStruct((bsz, t, DA_WIDTH), BF16),
        scratch_shapes=[pltpu.VMEM((2 * tq, LANES), BF16),
                        pltpu.VMEM((2 * tq, tk), F32),
                        pltpu.VMEM((2 * tq, tk), F32),
                        pltpu.VMEM((2 * tq, tk), BF16),
                        pltpu.VMEM((2 * tq, tk), BF16),
                        pltpu.VMEM((2 * tq, LANES), F32),
                        pltpu.VMEM((2 * tq, 2 * LANES), F32)],
        compiler_params=_cparams(("parallel", "parallel", "arbitrary")),
        name="attn",
    )(lam, qkv, qkv, qkv, subln)


HGRN_SUB_LEVELS = 3


def _hgrn_consts(c, d):
    nl = int(math.log2(c))
    t = np.arange(c)
    pi = t if d == 0 else c - 1 - t
    pr, pu = pi[:, None], pi[None, :]
    lm = np.zeros(((HGRN_SUB_LEVELS + 1) * c, c), np.float32)
    for lg in range(HGRN_SUB_LEVELS):
        same = (pr >> lg) == (pu >> lg)
        sec = ((pr >> lg) & 1) == 1
        lm[lg * c:(lg + 1) * c] = same & np.where(sec, pu <= pr, pu > pr)
    lm[HGRN_SUB_LEVELS * c:] = pu <= pr
    masks = np.zeros((nl + 1, c, c), np.float32)
    for lg in range(nl):
        masks[lg] = (t[:, None] >> (lg + 1)) == (t[None, :] >> (lg + 1))
    masks[nl] = np.eye(c)
    return jnp.asarray(lm, BF16), jnp.asarray(masks, F32), nl


def _hgrn_kernel(z_ref, i_ref, q_ref, lb_ref, lmat_ref, mask_ref, o_ref, st_ref, *, c, nl, d):
    s = pl.program_id(1)
    heads = range(HG_HEADS)

    @pl.when(s == 0)
    def _():
        st_ref[...] = jnp.zeros(st_ref.shape, F32)

    def col(h):
        return slice(h * LANES, (h + 1) * LANES)

    f = [lb_ref[0, h] + (1.0 - lb_ref[0, h]) * jax.nn.sigmoid(z_ref[0, :, col(h)]) for h in heads]
    lf = [jnp.log(f[h]) for h in heads]
    kk = [1.0 - f[h] for h in heads]
    v = [i_ref[0, :, col(h)] for h in heads]
    q = [q_ref[0, :, col(h)].astype(F32) for h in heads]
    x = []
    for h in heads:
        hi = lf[h].astype(BF16)
        lo = (lf[h] - hi.astype(F32)).astype(BF16)
        x2 = _dot(lmat_ref[...], jnp.concatenate([hi, lo], axis=-1))
        x.append(x2[:, 0:LANES] + x2[:, LANES:2 * LANES])
    cum = [x[h][HGRN_SUB_LEVELS * c:] for h in heads]
    last = c - 1 if d == 0 else 0
    bend = [cum[h][last:last + 1] for h in heads]

    row = lax.broadcasted_iota(jnp.int32, (c, LANES), 0)
    scores = [mask_ref[nl] * _dot_nt(q[h].astype(BF16), kk[h].astype(BF16)) for h in heads]
    for lg in range(nl):
        hs = 1 << lg
        sec = ((row >> lg) & 1) == (1 - d)
        for h in heads:
            if lg < HGRN_SUB_LEVELS:
                xl = x[h][lg * c:(lg + 1) * c]
            else:
                mid = hs - 1 if d == 0 else hs
                c3 = cum[h].reshape(c // (2 * hs), 2 * hs, LANES)
                ref = jnp.broadcast_to(c3[:, mid:mid + 1, :], c3.shape).reshape(c, LANES)
                xl = -jnp.abs(cum[h] - ref)
            g = jnp.exp(xl)
            qt = jnp.where(sec, q[h] * g, 0.0).astype(BF16)
            kt = jnp.where(sec, 0.0, kk[h] * g).astype(BF16)
            scores[h] = scores[h] + mask_ref[lg] * _dot_nt(qt, kt)
    for h in heads:
        st = st_ref[h]
        o = (_dot_nt((q[h] * jnp.exp(cum[h])).astype(BF16), st.astype(BF16))
             + _dot(scores[h].astype(BF16), v[h]))
        kb = (kk[h] * jnp.exp(bend[h] - cum[h])).astype(BF16)
        st_ref[h] = st * jnp.exp(bend[h]) + _dot_tn(v[h], kb)
        o_ref[0, :, col(h)] = o.astype(o_ref.dtype)


def _hgrn_call(z, iqg, lb4, c, d):
    bsz, rows, _ = z.shape
    nb = rows // c - 1
    lmat, masks, nl = _hgrn_consts(c, d)

    def blk(s):
        return jnp.where(s == 0, nb, s - 1 if d == 0 else nb - s)

    kern = functools.partial(_hgrn_kernel, c=c, nl=nl, d=d)
    return pl.pallas_call(
        kern,
        grid=(bsz, nb + 1),
        in_specs=[pl.BlockSpec((1, c, HG_WIDTH), lambda b, s: (b, blk(s), d)),
                  pl.BlockSpec((1, c, HG_WIDTH), lambda b, s: (b, blk(s), 0)),
                  pl.BlockSpec((1, c, HG_WIDTH), lambda b, s: (b, blk(s), 1)),
                  pl.BlockSpec((1, HG_HEADS, 1, LANES), lambda b, s: (d, 0, 0, 0)),
                  pl.BlockSpec(lmat.shape, lambda b, s: (0, 0)),
                  pl.BlockSpec(masks.shape, lambda b, s: (0, 0, 0))],
        out_specs=pl.BlockSpec((1, c, HG_WIDTH), lambda b, s: (b, blk(s), 0)),
        out_shape=jax.ShapeDtypeStruct((bsz, rows, HG_WIDTH), BF16),
        scratch_shapes=[pltpu.VMEM((HG_HEADS, HG_HEAD_DIM, HG_HEAD_DIM), F32)],
        compiler_params=_cparams(("parallel", "arbitrary")),
        name="hgrn_bwd" if d else "hgrn_fwd",
    )(z, iqg, iqg, lb4, lmat, masks)


def _post_kernel(da_ref, of_ref, ob_ref, g_ref, x_ref, mod_ref, hgn_ref, wout_ref, npost_ref, npre_ref,
                 wrh_ref, wrl_ref, x1_ref, h2_ref, aff_ref, *, d):
    b = pl.program_id(0)
    o = of_ref[0].astype(F32) + ob_ref[0].astype(F32)
    parts = []
    for h in range(HG_HEADS):
        oh = o[:, h * LANES:(h + 1) * LANES]
        ms = jnp.mean(oh * oh, axis=-1, keepdims=True)
        parts.append(oh * lax.rsqrt(ms + NORM_EPS) * hgn_ref[...])
    g = g_ref[0].astype(F32)
    ohn = jnp.concatenate(parts, axis=-1) * (g * jax.nn.sigmoid(g))
    mixed = jnp.concatenate([da_ref[0], ohn.astype(BF16)], axis=-1)
    y = _dot(mixed, wout_ref[...])
    yn = y * lax.rsqrt(jnp.mean(y * y, axis=-1, keepdims=True) + NORM_EPS) * npost_ref[...]
    gt1 = mod_ref[pl.ds(b, 1), 2 * d:3 * d]
    sh2 = mod_ref[pl.ds(b, 1), 3 * d:4 * d]
    sc2 = mod_ref[pl.ds(b, 1), 4 * d:5 * d]
    x1 = x_ref[0] + gt1 * yn
    x1_ref[0] = x1
    h2 = x1 * lax.rsqrt(jnp.mean(x1 * x1, axis=-1, keepdims=True) + NORM_EPS) * npre_ref[...]
    h2 = h2 * (1.0 + sc2) + sh2
    h2_ref[0] = h2
    hh = h2.astype(BF16)
    hl = (h2 - hh.astype(F32)).astype(BF16)
    logits = _dot_nt(wrh_ref[...], hh) + (_dot_nt(wrh_ref[...], hl) + _dot_nt(wrl_ref[...], hh))
    mx = jnp.max(logits, axis=0, keepdims=True)
    e = jnp.exp(logits - mx)
    aff_ref[0] = e / jnp.sum(e, axis=0, keepdims=True)


def _post_call(out_da, o_fwd, o_bwd, iqg, x, mod, hg_norm, w_out, npost, npre, wr_hi, wr_lo, tm):
    bsz, t, d = x.shape
    kern = functools.partial(_post_kernel, d=d)
    row = lambda b, i: (0, 0)
    return pl.pallas_call(
        kern,
        grid=(bsz, t // tm),
        in_specs=[pl.BlockSpec((1, tm, DA_WIDTH), lambda b, i: (b, i, 0)),
                  pl.BlockSpec((1, tm, HG_WIDTH), lambda b, i: (b, i, 0)),
                  pl.BlockSpec((1, tm, HG_WIDTH), lambda b, i: (b, i, 0)),
                  pl.BlockSpec((1, tm, HG_WIDTH), lambda b, i: (b, i, 2)),
                  pl.BlockSpec((1, tm, d), lambda b, i: (b, i, 0)),
                  pl.BlockSpec(mod.shape, row),
                  pl.BlockSpec((1, LANES), row),
                  pl.BlockSpec(w_out.shape, row),
                  pl.BlockSpec((1, d), row),
                  pl.BlockSpec((1, d), row),
                  pl.BlockSpec(wr_hi.shape, row),
                  pl.BlockSpec(wr_lo.shape, row)],
        out_specs=[pl.BlockSpec((1, tm, d), lambda b, i: (b, i, 0)),
                   pl.BlockSpec((1, tm, d), lambda b, i: (b, i, 0)),
                   pl.BlockSpec((1, N_EXPERTS, tm), lambda b, i: (b, 0, i))],
        out_shape=[jax.ShapeDtypeStruct((bsz, t, d), F32),
                   jax.ShapeDtypeStruct((bsz, t, d), F32),
                   jax.ShapeDtypeStruct((bsz, N_EXPERTS, t), F32)],
        compiler_params=_cparams(("parallel", "arbitrary")),
        name="post",
    )(out_da, o_fwd, o_bwd, iqg, x, mod, hg_norm, w_out, npost, npre, wr_hi, wr_lo)


TOPK_BLOCK = 256
TOPK_COUNT_BLOCK = 1024


def _topk_kernel(aff_ref, utri_ref, idx_ref, gate_ref, selc_ref, cnt_ref, cnt_smem, acci_ref, accg_ref,
                 *, t, cap):
    ne = N_EXPERTS
    cb = TOPK_BLOCK
    nblk = t // cb
    kb = min(TOPK_COUNT_BLOCK, t)

    def as_float(bits):
        return lax.bitcast_convert_type(bits, F32)

    def count_ge(thr):
        thr_b = jnp.broadcast_to(thr, (ne, kb))

        def body(i, acc):
            a = aff_ref[0, :, pl.ds(pl.multiple_of(i * kb, kb), kb)]
            return acc + jnp.where(a >= thr_b, 1, 0)
        acc = lax.fori_loop(0, t // kb, body, jnp.zeros((ne, kb), jnp.int32))
        return jnp.sum(acc, axis=1, keepdims=True)

    def bit_body(k, tau):
        cand = tau | (jnp.int32(1) << (29 - k))
        return jnp.where(count_ge(as_float(cand)) >= cap, cand, tau)
    tau = lax.fori_loop(0, 30, bit_body, jnp.zeros((ne, 1), jnp.int32))
    tau_f = as_float(tau)
    nxt_f = as_float(tau + 1)

    n_gt = count_ge(nxt_f)
    need = (cap - n_gt).astype(F32)

    utri = utri_ref[...]

    def sel_body(i, carry):
        ceq, csel = carry
        a = aff_ref[0, :, pl.ds(pl.multiple_of(i * cb, cb), cb)]
        gt = a >= nxt_f
        eq = (a >= tau_f) & jnp.logical_not(gt)
        eq_rank = _dot(jnp.where(eq, 1.0, 0.0).astype(BF16), utri) + ceq
        sel = gt | (eq & (eq_rank <= need))
        c_incl = _dot(jnp.where(sel, 1.0, 0.0).astype(BF16), utri) + csel
        selc_ref[:, pl.ds(pl.multiple_of(i * cb, cb), cb)] = jnp.where(sel, c_incl, 0.0)
        cnt_ref[i] = jnp.broadcast_to(csel, (ne, LANES)).astype(jnp.int32)
        return eq_rank[:, cb - 1:cb], c_incl[:, cb - 1:cb]
    zero = jnp.zeros((ne, 1), F32)
    _, c_all = lax.fori_loop(0, nblk, sel_body, (zero, zero))
    cnt_ref[nblk] = jnp.broadcast_to(c_all, (ne, LANES)).astype(jnp.int32)

    def to_smem(i, _):
        blk = cnt_ref[i]
        for e in range(ne):
            cnt_smem[i * ne + e] = blk[e, 0]
        return 0
    lax.fori_loop(0, nblk + 1, to_smem, 0)

    n_pt = cap // LANES
    slot = lax.broadcasted_iota(jnp.int32, (LANES, cb), 0)
    for e in range(ne):
        acci_ref[...] = jnp.zeros(acci_ref.shape, F32)
        accg_ref[...] = jnp.zeros(accg_ref.shape, F32)

        def slot_body(i, _):
            cols = pl.ds(pl.multiple_of(i * cb, cb), cb)
            cc = jnp.broadcast_to(selc_ref[e:e + 1, cols], (LANES, cb))
            aa = jnp.broadcast_to(aff_ref[0, e:e + 1, cols], (LANES, cb))
            tt = (lax.broadcasted_iota(jnp.int32, (LANES, cb), 1) + i * cb).astype(F32)
            first = cnt_smem[i * ne + e]
            last = cnt_smem[(i + 1) * ne + e] - 1

            def tile_body(j, _):
                hit = cc == (slot + (j * LANES + 1)).astype(F32)
                ti = jnp.where(hit, tt, 0.0)
                gi = jnp.where(hit, aa, 0.0)
                for c0 in range(0, cb, LANES):
                    acci_ref[j] = acci_ref[j] + ti[:, c0:c0 + LANES]
                    accg_ref[j] = accg_ref[j] + gi[:, c0:c0 + LANES]
                return 0
            lax.fori_loop(first // LANES, last // LANES + 1, tile_body, 0)
            return 0
        lax.fori_loop(0, nblk, slot_body, 0)
        for j in range(n_pt):
            idx_ref[0, e:e + 1, j * LANES:(j + 1) * LANES] = (
                jnp.sum(acci_ref[j].T, axis=0, keepdims=True).astype(jnp.int32))
            gate_ref[0, e:e + 1, j * LANES:(j + 1) * LANES] = jnp.sum(accg_ref[j].T, axis=0, keepdims=True)


def _topk_call(aff, cap):
    bsz, ne, t = aff.shape
    cb = TOPK_BLOCK
    utri = jnp.asarray(np.triu(np.ones((cb, cb), np.float32)), BF16)
    kern = functools.partial(_topk_kernel, t=t, cap=cap)
    return pl.pallas_call(
        kern,
        grid=(bsz,),
        in_specs=[pl.BlockSpec((1, ne, t), lambda b: (b, 0, 0)),
                  pl.BlockSpec((cb, cb), lambda b: (0, 0))],
        out_specs=[pl.BlockSpec((1, ne, cap), lambda b: (b, 0, 0)),
                   pl.BlockSpec((1, ne, cap), lambda b: (b, 0, 0))],
        out_shape=[jax.ShapeDtypeStruct((bsz, ne, cap), jnp.int32),
                   jax.ShapeDtypeStruct((bsz, ne, cap), F32)],
        scratch_shapes=[pltpu.VMEM((ne, t), F32),
                        pltpu.VMEM((t // cb + 1, ne, LANES), jnp.int32),
                        pltpu.SMEM(((t // cb + 1) * ne,), jnp.int32),
                        pltpu.VMEM((cap // LANES, LANES, LANES), F32),
                        pltpu.VMEM((cap // LANES, LANES, LANES), F32)],
        compiler_params=_cparams(("arbitrary",)),
        name="topk",
    )(aff, utri)


def _moe_kernel(idx_ref, h2_ref, gate_ref, wg_ref, wu_ref, wd_ref, x1_ref, mod_ref, npost_ref, o_ref,
                f_scr, xin, xbf, yacc, xbuf, obuf, gsem, xsem, osem, *, t, cap, ne, nf, zr):
    b = pl.program_id(0)
    e = pl.program_id(1)
    fc = pl.program_id(2)
    nb = pl.num_programs(0)
    flat = b * ne + e
    base = flat * cap
    slot = flat % 2
    last = nb * ne - 1
    nxt = jnp.minimum(flat + 1, last)
    rows_per_fc = cap // nf

    @pl.when((e == 0) & (fc == 0))
    def _():
        def zbody(i, _):
            f_scr[pl.ds(i * zr, zr), :] = jnp.zeros((zr, f_scr.shape[1]), F32)
            return 0
        lax.fori_loop(0, t // zr, zbody, 0)
        yacc[...] = jnp.zeros(yacc.shape, F32)

    def row_copy(step, dst, p):
        tok = idx_ref[step * cap + p]
        return pltpu.make_async_copy(h2_ref.at[step // ne, pl.ds(tok, 1), :],
                                     xin.at[dst, pl.ds(p, 1), :], gsem.at[dst])

    def wait_rows(dst):
        pltpu.make_async_copy(h2_ref.at[0, pl.ds(0, cap), :], xin.at[dst], gsem.at[dst]).wait()

    @pl.when((flat == 0) & (fc == 0))
    def _():
        def gstart(p, _):
            row_copy(flat, slot, p).start()
            return 0
        lax.fori_loop(0, cap, gstart, 0, unroll=8)

    @pl.when(fc == 0)
    def _():
        wait_rows(slot)
        xbf[...] = xin[slot].astype(BF16)

    for p in range(rows_per_fc):
        row_copy(nxt, 1 - slot, fc * rows_per_fc + p).start()

    wgb = wg_ref[0].astype(BF16)
    wub = wu_ref[0].astype(BF16)
    wdb = wd_ref[0].astype(BF16)
    half = cap // 2
    first = fc == 0

    def gate_up(rows):
        x = xbf[rows, :]
        return _dot(x, wgb), _dot(x, wub)

    def swiglu(gu):
        g, u = gu
        return (g * jax.nn.sigmoid(g) * u).astype(BF16)

    def down(rows, hmid):
        contrib = _dot(hmid, wdb)
        yacc[rows, :] = jnp.where(first, contrib, yacc[rows, :] + contrib)

    rows_a, rows_b = slice(0, half), slice(half, cap)
    gu_a = gate_up(rows_a)
    h_a = swiglu(gu_a)
    gu_b = gate_up(rows_b)
    down(rows_a, h_a)
    h_b = swiglu(gu_b)
    down(rows_b, h_b)

    @pl.when(fc == nf - 1)
    def _():
        yacc[...] = yacc[...] * gate_ref[0, 0]

        def sbody(g, _):
            p0 = g * SCATTER_GROUP
            toks = [idx_ref[base + p0 + i] for i in range(SCATTER_GROUP)]
            yr = [yacc[pl.ds(p0 + i, 1), :] for i in range(SCATTER_GROUP)]
            fr = [f_scr[pl.ds(toks[i], 1), :] for i in range(SCATTER_GROUP)]
            for i in range(SCATTER_GROUP):
                f_scr[pl.ds(toks[i], 1), :] = fr[i] + yr[i]
            return 0
        lax.fori_loop(0, cap // SCATTER_GROUP, sbody, 0)

    @pl.when((e == ne - 1) & (fc == nf - 1))
    def _():
        d = f_scr.shape[1]
        nblk = t // zr
        gt2 = mod_ref[pl.ds(b, 1), 5 * d:6 * d]

        def x_copy(i, s):
            return pltpu.make_async_copy(x1_ref.at[b, pl.ds(i * zr, zr), :], xbuf.at[s], xsem.at[s])

        def o_copy(i, s):
            return pltpu.make_async_copy(obuf.at[s], o_ref.at[b, pl.ds(i * zr, zr), :], osem.at[s])

        x_copy(0, 0).start()

        def fbody(i, _):
            s = i % 2
            x_copy(i, s).wait()

            @pl.when(i + 1 < nblk)
            def _():
                x_copy(i + 1, 1 - s).start()

            @pl.when(i >= 2)
            def _():
                o_copy(i - 2, s).wait()
            f = f_scr[pl.ds(i * zr, zr), :]
            fn = f * lax.rsqrt(jnp.mean(f * f, axis=-1, keepdims=True) + NORM_EPS) * npost_ref[...]
            obuf[s] = xbuf[s] + gt2 * fn
            o_copy(i, s).start()
            return 0
        lax.fori_loop(0, nblk, fbody, 0)
        o_copy(nblk - 2, nblk % 2).wait()
        o_copy(nblk - 1, (nblk - 1) % 2).wait()

    @pl.when((flat == last) & (fc == nf - 1))
    def _():
        wait_rows(1 - slot)


def _moe_call(idx_flat, h2, gate4, w_gate, w_up, w_down, x1, mod, npost, cap, tf):
    bsz, t, d = h2.shape
    ne, _, ff = w_gate.shape
    nf = ff // tf
    zr = 256
    kern = functools.partial(_moe_kernel, t=t, cap=cap, ne=ne, nf=nf, zr=zr)
    grid_spec = pltpu.PrefetchScalarGridSpec(
        num_scalar_prefetch=1,
        grid=(bsz, ne, nf),
        in_specs=[pl.BlockSpec(memory_space=pl.ANY),
                  pl.BlockSpec((1, 1, cap, 1), lambda b, e, f, idx: (b, e, 0, 0)),
                  pl.BlockSpec((1, d, tf), lambda b, e, f, idx: (e, 0, f)),
                  pl.BlockSpec((1, d, tf), lambda b, e, f, idx: (e, 0, f)),
                  pl.BlockSpec((1, tf, d), lambda b, e, f, idx: (e, f, 0)),
                  pl.BlockSpec(memory_space=pl.ANY),
                  pl.BlockSpec(mod.shape, lambda b, e, f, idx: (0, 0)),
                  pl.BlockSpec((1, d), lambda b, e, f, idx: (0, 0))],
        out_specs=pl.BlockSpec(memory_space=pl.ANY),
        scratch_shapes=[pltpu.VMEM((t, d), F32),
                        pltpu.VMEM((2, cap, d), F32),
                        pltpu.VMEM((cap, d), BF16),
                        pltpu.VMEM((cap, d), F32),
                        pltpu.VMEM((2, zr, d), F32),
                        pltpu.VMEM((2, zr, d), F32),
                        pltpu.SemaphoreType.DMA((2,)),
                        pltpu.SemaphoreType.DMA((2,)),
                        pltpu.SemaphoreType.DMA((2,))],
    )
    return pl.pallas_call(
        kern,
        grid_spec=grid_spec,
        out_shape=jax.ShapeDtypeStruct((bsz, t, d), F32),
        compiler_params=_cparams(("arbitrary", "arbitrary", "arbitrary")),
        name="moe",
    )(idx_flat, h2, gate4, w_gate, w_up, w_down, x1, mod, npost)


def _rope_tables(t, ctx_len):
    half = DA_HEAD_DIM // 2
    inv_freq = 1.0 / (ROPE_THETA ** (jnp.arange(0, half, 2, dtype=F32) / half))
    pos = jnp.arange(t, dtype=jnp.int32)
    r = (pos // GRID_W).astype(F32)
    col = (pos % GRID_W).astype(F32)
    ang_r = r[:, None] * inv_freq
    ang_c = col[:, None] * inv_freq
    ang = jnp.concatenate([ang_r, ang_r, ang_c, ang_c], axis=-1)
    cos = jnp.concatenate([jnp.cos(ang), jnp.ones((ctx_len, DA_HEAD_DIM), F32)], axis=0)
    sin = jnp.concatenate([jnp.sin(ang), jnp.zeros((ctx_len, DA_HEAD_DIM), F32)], axis=0)
    quarter = DA_HEAD_DIM // 4
    sign = jnp.where((jnp.arange(DA_HEAD_DIM) % (2 * quarter)) < quarter, -1.0, 1.0).astype(F32)
    return jnp.tile(cos, (1, 2)), jnp.tile(sin * sign, (1, 2))


def kernel(x, c, ctx, c_ctx, w_ada, b_ada, norm_pre_mix, norm_post_mix, norm_pre_ffn, norm_post_ffn, w_in,
           da_lambda_q1, da_lambda_k1, da_lambda_q2, da_lambda_k2, da_subln, hg_lower_bound, hg_norm, w_out,
           w_router, w_gate, w_up, w_down):
    bsz, t, d = x.shape
    ctx_len = ctx.shape[1]
    assert bsz <= 2 and d == 1024 and w_ada.shape[0] == 1
    cap = CAPACITY_FACTOR * t // N_EXPERTS
    lambda_init = 0.8 - 0.6 * math.exp(-0.3 * 0)

    lam = (jnp.exp(jnp.sum(da_lambda_q1[0].astype(F32) * da_lambda_k1[0].astype(F32)))
           - jnp.exp(jnp.sum(da_lambda_q2[0].astype(F32) * da_lambda_k2[0].astype(F32))) + lambda_init)
    lam = lam.reshape(1, 1)
    lb = jnp.cumsum(jax.nn.softmax(hg_lower_bound.astype(F32), axis=0), axis=0)[0]
    lb4 = lb.reshape(2, HG_HEADS, 1, HG_HEAD_DIM)

    wi = w_in[0]
    scale = DA_HEAD_DIM ** -0.5 * math.log2(math.e)
    wq = wi[:, 0:DA_WIDTH] * scale
    w_all = jnp.concatenate([wq, wi[:, DA_WIDTH:]], axis=1).astype(BF16)
    cos, sin = _rope_tables(t, ctx_len)

    cmat = jnp.concatenate([c, c_ctx[None, :], jnp.zeros((8 - bsz - 1, d), F32)], axis=0)
    mod = _mod_call(cmat, w_ada[0], b_ada[0][None, :])

    qkv, z, iqg = _inproj_call(x, ctx, mod, norm_pre_mix[0][None, :], w_all, cos, sin)

    tq = min(ATTN_TQ, t)
    nk = t + ctx_len
    tk = ATTN_TK if nk % ATTN_TK == 0 else ctx_len
    out_da = _attn_call(lam, qkv, da_subln[0][None, :], t, tq, tk, 1.0 - lambda_init)
    o_fwd = _hgrn_call(z, iqg, lb4, ctx_len, 0)
    o_bwd = _hgrn_call(z, iqg, lb4, ctx_len, 1)

    wr = w_router[0].T
    wr_hi = wr.astype(BF16)
    wr_lo = (wr - wr_hi.astype(F32)).astype(BF16)
    x1, h2, aff = _post_call(out_da, o_fwd, o_bwd, iqg, x, mod, hg_norm[0][None, :], w_out[0].astype(BF16),
                             norm_post_mix[0][None, :], norm_pre_ffn[0][None, :], wr_hi, wr_lo, POST_TM)
    idx, gate = _topk_call(aff, cap)
    return _moe_call(idx.reshape(-1), h2, gate.reshape(bsz, N_EXPERTS, cap, 1), w_gate[0], w_up[0], w_down[0],
                     x1, mod, norm_post_ffn[0][None, :], cap, 256)
```

```python
import functools
import math

import jax
import jax.numpy as jnp
import numpy as np
from jax import lax
from jax.experimental import pallas as pl
from jax.experimental.pallas import tpu as pltpu

F32 = jnp.float32
BF16 = jnp.bfloat16

GRID_W = 64
DA_HEADS = 4
DA_HEAD_DIM = 64
DA_V_DIM = 128
DA_WIDTH = 512
HG_HEADS = 4
HG_HEAD_DIM = 128
HG_WIDTH = 512
N_EXPERTS = 16
CAPACITY_FACTOR = 2
ROPE_THETA = 10000.0
NORM_EPS = 1e-6
LANES = 128
VMEM_LIMIT = 60 * 1024 * 1024
ATTN_TQ = 1024
ATTN_TK = 768
ATTN_ROW_BLOCK = 128
POST_TM = 512
SCATTER_GROUP = 8


def _cparams(sem, vmem=VMEM_LIMIT):
    return pltpu.CompilerParams(dimension_semantics=sem, vmem_limit_bytes=vmem)


def _dot(a, b):
    return jnp.dot(a, b, preferred_element_type=F32)


def _dot_nt(a, b):
    return lax.dot_general(a, b, (((1,), (1,)), ((), ())), preferred_element_type=F32)


def _dot_tn(a, b):
    return lax.dot_general(a, b, (((0,), (0,)), ((), ())), preferred_element_type=F32)


def _split3(x):
    hi = x.astype(BF16)
    r1 = x - hi.astype(F32)
    mid = r1.astype(BF16)
    lo = (r1 - mid.astype(F32)).astype(BF16)
    return hi, mid, lo


def _mod_kernel(s_ref, w_ref, b_ref, o_ref):
    s = s_ref[...]
    s = s * jax.nn.sigmoid(s)
    hi, mid, lo = _split3(s)
    w = w_ref[...]
    whi, wmid, wlo = _split3(w)
    acc = _dot(hi, whi) + (_dot(hi, wmid) + _dot(mid, whi)) + (_dot(hi, wlo) + _dot(mid, wmid) + _dot(lo, whi))
    o_ref[...] = acc + b_ref[...]


def _mod_call(cmat, w_ada, b_ada):
    d = cmat.shape[1]
    n = w_ada.shape[1]
    tn = 1536 if n % 1536 == 0 else n
    return pl.pallas_call(
        _mod_kernel,
        grid=(n // tn,),
        in_specs=[pl.BlockSpec((8, d), lambda j: (0, 0)),
                  pl.BlockSpec((d, tn), lambda j: (0, j)),
                  pl.BlockSpec((1, tn), lambda j: (0, j))],
        out_specs=pl.BlockSpec((8, tn), lambda j: (0, j)),
        out_shape=jax.ShapeDtypeStruct((8, n), F32),
        compiler_params=_cparams(("arbitrary",)),
        name="mod",
    )(cmat, w_ada, b_ada)


def _inproj_kernel(x_ref, ctx_ref, mod_ref, g_ref, w_ref, cos_ref, sin_ref, qkv_ref, z_ref, iqg_ref, *, n_lat, d):
    b = pl.program_id(0)
    i = pl.program_id(1)
    is_ctx = i == n_lat
    xb = jnp.where(is_ctx, ctx_ref[0], x_ref[0])
    row = jnp.where(is_ctx, 2, b)
    sh = mod_ref[pl.ds(row, 1), 0:d]
    sc = mod_ref[pl.ds(row, 1), d:2 * d]
    ms = jnp.mean(xb * xb, axis=-1, keepdims=True)
    h = xb * lax.rsqrt(ms + NORM_EPS) * g_ref[...]
    h = h * (1.0 + sc) + sh
    p = _dot(h.astype(BF16), w_ref[...])
    quarter = DA_HEAD_DIM // 4
    lane = lax.broadcasted_iota(jnp.int32, (p.shape[0], LANES), 1)
    first = (lane % (2 * quarter)) < quarter
    qk = []
    for j in range(2 * DA_HEADS):
        xj = p[:, j * LANES:(j + 1) * LANES]
        partner = jnp.where(first, pltpu.roll(xj, LANES - quarter, axis=1), pltpu.roll(xj, quarter, axis=1))
        qk.append(xj * cos_ref[...] + partner * sin_ref[...])
    qk = jnp.concatenate(qk, axis=-1)
    for j in range(2 * DA_HEADS):
        qkv_ref[0, j] = qk[:, j * LANES:(j + 1) * LANES].astype(BF16)
    for j in range(DA_HEADS):
        qkv_ref[0, 2 * DA_HEADS + j] = p[:, 1024 + j * LANES:1024 + (j + 1) * LANES].astype(BF16)
    z_ref[0] = p[:, 1536:2560]
    iqg_ref[0] = p[:, 2560:4096].astype(BF16)


def _inproj_call(x, ctx, mod, g, w, cos, sin):
    bsz, t, d = x.shape
    tm = ctx.shape[1]
    n_lat = t // tm
    rows = t + tm
    kern = functools.partial(_inproj_kernel, n_lat=n_lat, d=d)
    return pl.pallas_call(
        kern,
        grid=(bsz, n_lat + 1),
        in_specs=[pl.BlockSpec((1, tm, d), lambda b, i: (b, jnp.minimum(i, n_lat - 1), 0)),
                  pl.BlockSpec((1, tm, d), lambda b, i: (b, 0, 0)),
                  pl.BlockSpec(mod.shape, lambda b, i: (0, 0)),
                  pl.BlockSpec((1, d), lambda b, i: (0, 0)),
                  pl.BlockSpec(w.shape, lambda b, i: (0, 0)),
                  pl.BlockSpec((tm, LANES), lambda b, i: (i, 0)),
                  pl.BlockSpec((tm, LANES), lambda b, i: (i, 0))],
        out_specs=[pl.BlockSpec((1, 3 * DA_HEADS, tm, LANES), lambda b, i: (b, 0, i, 0)),
                   pl.BlockSpec((1, tm, 1024), lambda b, i: (b, i, 0)),
                   pl.BlockSpec((1, tm, 1536), lambda b, i: (b, i, 0))],
        out_shape=[jax.ShapeDtypeStruct((bsz, 3 * DA_HEADS, rows, LANES), BF16),
                   jax.ShapeDtypeStruct((bsz, rows, 1024), F32),
                   jax.ShapeDtypeStruct((bsz, rows, 1536), BF16)],
        compiler_params=_cparams(("parallel", "arbitrary")),
        name="inproj",
    )(x, ctx, mod, g, w, cos, sin)


def _attn_kernel(lam_ref, q_ref, k_ref, v_ref, subln_ref, o_ref, qs_ref, sa_ref, sb_ref, pa_ref, pb_ref, m_ref,
                 acc_ref, *, tq, tk, nchunk, out_scale):
    q = q_ref[0, 0]
    lane = lax.broadcasted_iota(jnp.int32, q.shape, 1)
    zero = jnp.zeros_like(q)
    qs_ref[0:tq, :] = jnp.where(lane < DA_HEAD_DIM, q, zero)
    qs_ref[tq:2 * tq, :] = jnp.where(lane >= DA_HEAD_DIM, q, zero)

    m_ref[...] = jnp.full(m_ref.shape, -1e30, F32)
    acc_ref[...] = jnp.zeros(acc_ref.shape, F32)

    def scores(j, s_ref):
        kj = k_ref[0, 0, pl.ds(pl.multiple_of(j * tk, tk), tk), :]
        s_ref[...] = _dot_nt(qs_ref[...], kj)

    def update(j, s_ref, p_ref):
        vj = v_ref[0, 0, pl.ds(pl.multiple_of(j * tk, tk), tk), :]
        vext = jnp.concatenate([vj, jnp.ones_like(vj)], axis=-1)
        z = jnp.minimum(pl.program_id(2), 0)

        def s_at(r0, nrows, cols):
            return s_ref[pl.ds(pl.multiple_of(r0 + z, 8), nrows), cols]
        m = m_ref[...]
        m_new = jnp.maximum(m, jnp.broadcast_to(jnp.max(s_at(0, 2 * tq, slice(None)), axis=-1, keepdims=True),
                                                m.shape))
        alpha = jnp.exp2(m - m_new)
        m_ref[...] = m_new
        rb = min(ATTN_ROW_BLOCK, 2 * tq)
        for r0 in range(0, 2 * tq, rb):
            rows = slice(r0, r0 + rb)
            mb = m_ref[rows, :]
            for c0 in range(0, tk, LANES):
                p_ref[rows, c0:c0 + LANES] = jnp.exp2((s_at(r0, rb, slice(c0, c0 + LANES)) - mb).astype(BF16))
        acc_ref[...] = jnp.tile(alpha, (1, 2)) * acc_ref[...] + _dot(p_ref[...], vext)

    npairs = (nchunk - 1) // 2
    scores(0, sa_ref)

    def body(i, _):
        scores(2 * i + 1, sb_ref)
        update(2 * i, sa_ref, pa_ref)
        scores(2 * i + 2, sa_ref)
        update(2 * i + 1, sb_ref, pb_ref)
        return 0
    lax.fori_loop(0, npairs, body, 0)
    if nchunk - 1 == 2 * npairs:
        update(nchunk - 1, sa_ref, pa_ref)
    else:
        scores(nchunk - 1, sb_ref)
        update(nchunk - 2, sa_ref, pa_ref)
        update(nchunk - 1, sb_ref, pb_ref)

    r = 1.0 / acc_ref[:, LANES:LANES + 1]
    acc = acc_ref[:, 0:LANES]
    o = acc[0:tq] * r[0:tq] - lam_ref[0, 0] * (acc[tq:2 * tq] * r[tq:2 * tq])
    ms = jnp.mean(o * o, axis=-1, keepdims=True)
    o = o * lax.rsqrt(ms + NORM_EPS) * subln_ref[...] * out_scale
    o_ref[0] = o.astype(o_ref.dtype)


def _attn_call(lam, qkv, subln, t, tq, tk, out_scale):
    bsz, _, rows, _ = qkv.shape
    kern = functools.partial(_attn_kernel, tq=tq, tk=tk, nchunk=rows // tk, out_scale=out_scale)
    return pl.pallas_call(
        kern,
        grid=(bsz, DA_HEADS, t // tq),
        in_specs=[pl.BlockSpec(memory_space=pltpu.SMEM),
                  pl.BlockSpec((1, 1, tq, LANES), lambda b, h, i: (b, h, i, 0)),
                  pl.BlockSpec((1, 1, rows, LANES), lambda b, h, i: (b, DA_HEADS + h, 0, 0)),
                  pl.BlockSpec((1, 1, rows, LANES), lambda b, h, i: (b, 2 * DA_HEADS + h, 0, 0)),
                  pl.BlockSpec((1, LANES), lambda b, h, i: (0, 0))],
        out_specs=pl.BlockSpec((1, tq, LANES), lambda b, h, i: (b, i, h)),
        out_shape=jax.ShapeDtypeStruct((bsz, t, DA_WIDTH), BF16),
        scratch_shapes=[pltpu.VMEM((2 * tq, LANES), BF16),
                        pltpu.VMEM((2 * tq, tk), F32),
                        pltpu.VMEM((2 * tq, tk), F32),
                        pltpu.VMEM((2 * tq, tk), BF16),
                        pltpu.VMEM((2 * tq, tk), BF16),
                        pltpu.VMEM((2 * tq, LANES), F32),
                        pltpu.VMEM((2 * tq, 2 * LANES), F32)],
        compiler_params=_cparams(("parallel", "parallel", "arbitrary")),
        name="attn",
    )(lam, qkv, qkv, qkv, subln)


HGRN_SUB_LEVELS = 3


def _hgrn_consts(c, d):
    nl = int(math.log2(c))
    t = np.arange(c)
    pi = t if d == 0 else c - 1 - t
    pr, pu = pi[:, None], pi[None, :]
    lm = np.zeros(((HGRN_SUB_LEVELS + 1) * c, c), np.float32)
    for lg in range(HGRN_SUB_LEVELS):
        same = (pr >> lg) == (pu >> lg)
        sec = ((pr >> lg) & 1) == 1
        lm[lg * c:(lg + 1) * c] = same & np.where(sec, pu <= pr, pu > pr)
    lm[HGRN_SUB_LEVELS * c:] = pu <= pr
    masks = np.zeros((nl + 1, c, c), np.float32)
    for lg in range(nl):
        masks[lg] = (t[:, None] >> (lg + 1)) == (t[None, :] >> (lg + 1))
    masks[nl] = np.eye(c)
    return jnp.asarray(lm, BF16), jnp.asarray(masks, F32), nl


def _hgrn_kernel(z_ref, i_ref, q_ref, lb_ref, lmat_ref, mask_ref, o_ref, st_ref, *, c, nl, d):
    s = pl.program_id(1)
    heads = range(HG_HEADS)

    @pl.when(s == 0)
    def _():
        st_ref[...] = jnp.zeros(st_ref.shape, F32)

    def col(h):
        return slice(h * LANES, (h + 1) * LANES)

    f = [lb_ref[0, h] + (1.0 - lb_ref[0, h]) * jax.nn.sigmoid(z_ref[0, :, col(h)]) for h in heads]
    lf = [jnp.log(f[h]) for h in heads]
    kk = [1.0 - f[h] for h in heads]
    v = [i_ref[0, :, col(h)] for h in heads]
    q = [q_ref[0, :, col(h)].astype(F32) for h in heads]
    x = []
    for h in heads:
        hi = lf[h].astype(BF16)
        lo = (lf[h] - hi.astype(F32)).astype(BF16)
        x2 = _dot(lmat_ref[...], jnp.concatenate([hi, lo], axis=-1))
        x.append(x2[:, 0:LANES] + x2[:, LANES:2 * LANES])
    cum = [x[h][HGRN_SUB_LEVELS * c:] for h in heads]
    last = c - 1 if d == 0 else 0
    bend = [cum[h][last:last + 1] for h in heads]

    row = lax.broadcasted_iota(jnp.int32, (c, LANES), 0)
    scores = [mask_ref[nl] * _dot_nt(q[h].astype(BF16), kk[h].astype(BF16)) for h in heads]
    for lg in range(nl):
        hs = 1 << lg
        sec = ((row >> lg) & 1) == (1 - d)
        for h in heads:
            if lg < HGRN_SUB_LEVELS:
                xl = x[h][lg * c:(lg + 1) * c]
            else:
                mid = hs - 1 if d == 0 else hs
                c3 = cum[h].reshape(c // (2 * hs), 2 * hs, LANES)
                ref = jnp.broadcast_to(c3[:, mid:mid + 1, :], c3.shape).reshape(c, LANES)
                xl = -jnp.abs(cum[h] - ref)
            g = jnp.exp(xl)
            qt = jnp.where(sec, q[h] * g, 0.0).astype(BF16)
            kt = jnp.where(sec, 0.0, kk[h] * g).astype(BF16)
            scores[h] = scores[h] + mask_ref[lg] * _dot_nt(qt, kt)
    for h in heads:
        st = st_ref[h]
        o = (_dot_nt((q[h] * jnp.exp(cum[h])).astype(BF16), st.astype(BF16))
             + _dot(scores[h].astype(BF16), v[h]))
        kb = (kk[h] * jnp.exp(bend[h] - cum[h])).astype(BF16)
        st_ref[h] = st * jnp.exp(bend[h]) + _dot_tn(v[h], kb)
        o_ref[0, :, col(h)] = o.astype(o_ref.dtype)


def _hgrn_call(z, iqg, lb4, c, d):
    bsz, rows, _ = z.shape
    nb = rows // c - 1
    lmat, masks, nl = _hgrn_consts(c, d)

    def blk(s):
        return jnp.where(s == 0, nb, s - 1 if d == 0 else nb - s)

    kern = functools.partial(_hgrn_kernel, c=c, nl=nl, d=d)
    return pl.pallas_call(
        kern,
        grid=(bsz, nb + 1),
        in_specs=[pl.BlockSpec((1, c, HG_WIDTH), lambda b, s: (b, blk(s), d)),
                  pl.BlockSpec((1, c, HG_WIDTH), lambda b, s: (b, blk(s), 0)),
                  pl.BlockSpec((1, c, HG_WIDTH), lambda b, s: (b, blk(s), 1)),
                  pl.BlockSpec((1, HG_HEADS, 1, LANES), lambda b, s: (d, 0, 0, 0)),
                  pl.BlockSpec(lmat.shape, lambda b, s: (0, 0)),
                  pl.BlockSpec(masks.shape, lambda b, s: (0, 0, 0))],
        out_specs=pl.BlockSpec((1, c, HG_WIDTH), lambda b, s: (b, blk(s), 0)),
        out_shape=jax.ShapeDtypeStruct((bsz, rows, HG_WIDTH), BF16),
        scratch_shapes=[pltpu.VMEM((HG_HEADS, HG_HEAD_DIM, HG_HEAD_DIM), F32)],
        compiler_params=_cparams(("parallel", "arbitrary")),
        name="hgrn_bwd" if d else "hgrn_fwd",
    )(z, iqg, iqg, lb4, lmat, masks)


def _post_kernel(da_ref, of_ref, ob_ref, g_ref, x_ref, mod_ref, hgn_ref, wout_ref, npost_ref, npre_ref,
                 wrh_ref, wrl_ref, x1_ref, h2_ref, aff_ref, *, d):
    b = pl.program_id(0)
    o = of_ref[0].astype(F32) + ob_ref[0].astype(F32)
    parts = []
    for h in range(HG_HEADS):
        oh = o[:, h * LANES:(h + 1) * LANES]
        ms = jnp.mean(oh * oh, axis=-1, keepdims=True)
        parts.append(oh * lax.rsqrt(ms + NORM_EPS) * hgn_ref[...])
    g = g_ref[0].astype(F32)
    ohn = jnp.concatenate(parts, axis=-1) * (g * jax.nn.sigmoid(g))
    mixed = jnp.concatenate([da_ref[0], ohn.astype(BF16)], axis=-1)
    y = _dot(mixed, wout_ref[...])
    yn = y * lax.rsqrt(jnp.mean(y * y, axis=-1, keepdims=True) + NORM_EPS) * npost_ref[...]
    gt1 = mod_ref[pl.ds(b, 1), 2 * d:3 * d]
    sh2 = mod_ref[pl.ds(b, 1), 3 * d:4 * d]
    sc2 = mod_ref[pl.ds(b, 1), 4 * d:5 * d]
    x1 = x_ref[0] + gt1 * yn
    x1_ref[0] = x1
    h2 = x1 * lax.rsqrt(jnp.mean(x1 * x1, axis=-1, keepdims=True) + NORM_EPS) * npre_ref[...]
    h2 = h2 * (1.0 + sc2) + sh2
    h2_ref[0] = h2
    hh = h2.astype(BF16)
    hl = (h2 - hh.astype(F32)).astype(BF16)
    logits = _dot_nt(wrh_ref[...], hh) + (_dot_nt(wrh_ref[...], hl) + _dot_nt(wrl_ref[...], hh))
    mx = jnp.max(logits, axis=0, keepdims=True)
    e = jnp.exp(logits - mx)
    aff_ref[0] = e / jnp.sum(e, axis=0, keepdims=True)


def _post_call(out_da, o_fwd, o_bwd, iqg, x, mod, hg_norm, w_out, npost, npre, wr_hi, wr_lo, tm):
    bsz, t, d = x.shape
    kern = functools.partial(_post_kernel, d=d)
    row = lambda b, i: (0, 0)
    return pl.pallas_call(
        kern,
        grid=(bsz, t // tm),
        in_specs=[pl.BlockSpec((1, tm, DA_WIDTH), lambda b, i: (b, i, 0)),
                  pl.BlockSpec((1, tm, HG_WIDTH), lambda b, i: (b, i, 0)),
                  pl.BlockSpec((1, tm, HG_WIDTH), lambda b, i: (b, i, 0)),
                  pl.BlockSpec((1, tm, HG_WIDTH), lambda b, i: (b, i, 2)),
                  pl.BlockSpec((1, tm, d), lambda b, i: (b, i, 0)),
                  pl.BlockSpec(mod.shape, row),
                  pl.BlockSpec((1, LANES), row),
                  pl.BlockSpec(w_out.shape, row),
                  pl.BlockSpec((1, d), row),
                  pl.BlockSpec((1, d), row),
                  pl.BlockSpec(wr_hi.shape, row),
                  pl.BlockSpec(wr_lo.shape, row)],
        out_specs=[pl.BlockSpec((1, tm, d), lambda b, i: (b, i, 0)),
                   pl.BlockSpec((1, tm, d), lambda b, i: (b, i, 0)),
                   pl.BlockSpec((1, N_EXPERTS, tm), lambda b, i: (b, 0, i))],
        out_shape=[jax.ShapeDtypeStruct((bsz, t, d), F32),
                   jax.ShapeDtypeStruct((bsz, t, d), F32),
                   jax.ShapeDtypeStruct((bsz, N_EXPERTS, t), F32)],
        compiler_params=_cparams(("parallel", "arbitrary")),
        name="post",
    )(out_da, o_fwd, o_bwd, iqg, x, mod, hg_norm, w_out, npost, npre, wr_hi, wr_lo)


TOPK_BLOCK = 256
TOPK_COUNT_BLOCK = 1024


def _topk_kernel(aff_ref, utri_ref, idx_ref, gate_ref, selc_ref, cnt_ref, cnt_smem, acci_ref, accg_ref,
                 *, t, cap):
    ne = N_EXPERTS
    cb = TOPK_BLOCK
    nblk = t // cb
    kb = min(TOPK_COUNT_BLOCK, t)

    def as_float(bits):
        return lax.bitcast_convert_type(bits, F32)

    def count_ge(thr):
        thr_b = jnp.broadcast_to(thr, (ne, kb))

        def body(i, acc):
            a = aff_ref[0, :, pl.ds(pl.multiple_of(i * kb, kb), kb)]
            return acc + jnp.where(a >= thr_b, 1, 0)
        acc = lax.fori_loop(0, t // kb, body, jnp.zeros((ne, kb), jnp.int32))
        return jnp.sum(acc, axis=1, keepdims=True)

    def bit_body(k, tau):
        cand = tau | (jnp.int32(1) << (29 - k))
        return jnp.where(count_ge(as_float(cand)) >= cap, cand, tau)
    tau = lax.fori_loop(0, 30, bit_body, jnp.zeros((ne, 1), jnp.int32))
    tau_f = as_float(tau)
    nxt_f = as_float(tau + 1)

    n_gt = count_ge(nxt_f)
    need = (cap - n_gt).astype(F32)

    utri = utri_ref[...]

    def sel_body(i, carry):
        ceq, csel = carry
        a = aff_ref[0, :, pl.ds(pl.multiple_of(i * cb, cb), cb)]
        gt = a >= nxt_f
        eq = (a >= tau_f) & jnp.logical_not(gt)
        eq_rank = _dot(jnp.where(eq, 1.0, 0.0).astype(BF16), utri) + ceq
        sel = gt | (eq & (eq_rank <= need))
        c_incl = _dot(jnp.where(sel, 1.0, 0.0).astype(BF16), utri) + csel
        selc_ref[:, pl.ds(pl.multiple_of(i * cb, cb), cb)] = jnp.where(sel, c_incl, 0.0)
        cnt_ref[i] = jnp.broadcast_to(csel, (ne, LANES)).astype(jnp.int32)
        return eq_rank[:, cb - 1:cb], c_incl[:, cb - 1:cb]
    zero = jnp.zeros((ne, 1), F32)
    _, c_all = lax.fori_loop(0, nblk, sel_body, (zero, zero))
    cnt_ref[nblk] = jnp.broadcast_to(c_all, (ne, LANES)).astype(jnp.int32)

    def to_smem(i, _):
        blk = cnt_ref[i]
        for e in range(ne):
            cnt_smem[i * ne + e] = blk[e, 0]
        return 0
    lax.fori_loop(0, nblk + 1, to_smem, 0)

    n_pt = cap // LANES
    slot = lax.broadcasted_iota(jnp.int32, (LANES, cb), 0)
    for e in range(ne):
        acci_ref[...] = jnp.zeros(acci_ref.shape, F32)
        accg_ref[...] = jnp.zeros(accg_ref.shape, F32)

        def slot_body(i, _):
            cols = pl.ds(pl.multiple_of(i * cb, cb), cb)
            cc = jnp.broadcast_to(selc_ref[e:e + 1, cols], (LANES, cb))
            aa = jnp.broadcast_to(aff_ref[0, e:e + 1, cols], (LANES, cb))
            tt = (lax.broadcasted_iota(jnp.int32, (LANES, cb), 1) + i * cb).astype(F32)
            first = cnt_smem[i * ne + e]
            last = cnt_smem[(i + 1) * ne + e] - 1

            def tile_body(j, _):
                hit = cc == (slot + (j * LANES + 1)).astype(F32)
                ti = jnp.where(hit, tt, 0.0)
                gi = jnp.where(hit, aa, 0.0)
                for c0 in range(0, cb, LANES):
                    acci_ref[j] = acci_ref[j] + ti[:, c0:c0 + LANES]
                    accg_ref[j] = accg_ref[j] + gi[:, c0:c0 + LANES]
                return 0
            lax.fori_loop(first // LANES, last // LANES + 1, tile_body, 0)
            return 0
        lax.fori_loop(0, nblk, slot_body, 0)
        for j in range(n_pt):
            idx_ref[0, e:e + 1, j * LANES:(j + 1) * LANES] = (
                jnp.sum(acci_ref[j].T, axis=0, keepdims=True).astype(jnp.int32))
            gate_ref[0, e:e + 1, j * LANES:(j + 1) * LANES] = jnp.sum(accg_ref[j].T, axis=0, keepdims=True)


def _topk_call(aff, cap):
    bsz, ne, t = aff.shape
    cb = TOPK_BLOCK
    utri = jnp.asarray(np.triu(np.ones((cb, cb), np.float32)), BF16)
    kern = functools.partial(_topk_kernel, t=t, cap=cap)
    return pl.pallas_call(
        kern,
        grid=(bsz,),
        in_specs=[pl.BlockSpec((1, ne, t), lambda b: (b, 0, 0)),
                  pl.BlockSpec((cb, cb), lambda b: (0, 0))],
        out_specs=[pl.BlockSpec((1, ne, cap), lambda b: (b, 0, 0)),
                   pl.BlockSpec((1, ne, cap), lambda b: (b, 0, 0))],
        out_shape=[jax.ShapeDtypeStruct((bsz, ne, cap), jnp.int32),
                   jax.ShapeDtypeStruct((bsz, ne, cap), F32)],
        scratch_shapes=[pltpu.VMEM((ne, t), F32),
                        pltpu.VMEM((t // cb + 1, ne, LANES), jnp.int32),
                        pltpu.SMEM(((t // cb + 1) * ne,), jnp.int32),
                        pltpu.VMEM((cap // LANES, LANES, LANES), F32),
                        pltpu.VMEM((cap // LANES, LANES, LANES), F32)],
        compiler_params=_cparams(("arbitrary",)),
        name="topk",
    )(aff, utri)


def _moe_kernel(idx_ref, h2_ref, gate_ref, wg_ref, wu_ref, wd_ref, x1_ref, mod_ref, npost_ref, o_ref,
                f_scr, xin, xbf, yacc, xbuf, obuf, gsem, xsem, osem, *, t, cap, ne, nf, zr):
    b = pl.program_id(0)
    e = pl.program_id(1)
    fc = pl.program_id(2)
    nb = pl.num_programs(0)
    flat = b * ne + e
    base = flat * cap
    slot = flat % 2
    last = nb * ne - 1
    nxt = jnp.minimum(flat + 1, last)
    rows_per_fc = cap // nf

    @pl.when((e == 0) & (fc == 0))
    def _():
        def zbody(i, _):
            f_scr[pl.ds(i * zr, zr), :] = jnp.zeros((zr, f_scr.shape[1]), F32)
            return 0
        lax.fori_loop(0, t // zr, zbody, 0)
        yacc[...] = jnp.zeros(yacc.shape, F32)

    def row_copy(step, dst, p):
        tok = idx_ref[step * cap + p]
        return pltpu.make_async_copy(h2_ref.at[step // ne, pl.ds(tok, 1), :],
                                     xin.at[dst, pl.ds(p, 1), :], gsem.at[dst])

    def wait_rows(dst):
        pltpu.make_async_copy(h2_ref.at[0, pl.ds(0, cap), :], xin.at[dst], gsem.at[dst]).wait()

    @pl.when((flat == 0) & (fc == 0))
    def _():
        def gstart(p, _):
            row_copy(flat, slot, p).start()
            return 0
        lax.fori_loop(0, cap, gstart, 0, unroll=8)

    @pl.when(fc == 0)
    def _():
        wait_rows(slot)
        xbf[...] = xin[slot].astype(BF16)

    for p in range(rows_per_fc):
        row_copy(nxt, 1 - slot, fc * rows_per_fc + p).start()

    wgb = wg_ref[0].astype(BF16)
    wub = wu_ref[0].astype(BF16)
    wdb = wd_ref[0].astype(BF16)
    half = cap // 2
    first = fc == 0

    def gate_up(rows):
        x = xbf[rows, :]
        return _dot(x, wgb), _dot(x, wub)

    def swiglu(gu):
        g, u = gu
        return (g * jax.nn.sigmoid(g) * u).astype(BF16)

    def down(rows, hmid):
        contrib = _dot(hmid, wdb)
        yacc[rows, :] = jnp.where(first, contrib, yacc[rows, :] + contrib)

    rows_a, rows_b = slice(0, half), slice(half, cap)
    gu_a = gate_up(rows_a)
    h_a = swiglu(gu_a)
    gu_b = gate_up(rows_b)
    down(rows_a, h_a)
    h_b = swiglu(gu_b)
    down(rows_b, h_b)

    @pl.when(fc == nf - 1)
    def _():
        yacc[...] = yacc[...] * gate_ref[0, 0]

        def sbody(g, _):
            p0 = g * SCATTER_GROUP
            toks = [idx_ref[base + p0 + i] for i in range(SCATTER_GROUP)]
            yr = [yacc[pl.ds(p0 + i, 1), :] for i in range(SCATTER_GROUP)]
            fr = [f_scr[pl.ds(toks[i], 1), :] for i in range(SCATTER_GROUP)]
            for i in range(SCATTER_GROUP):
                f_scr[pl.ds(toks[i], 1), :] = fr[i] + yr[i]
            return 0
        lax.fori_loop(0, cap // SCATTER_GROUP, sbody, 0)

    @pl.when((e == ne - 1) & (fc == nf - 1))
    def _():
        d = f_scr.shape[1]
        nblk = t // zr
        gt2 = mod_ref[pl.ds(b, 1), 5 * d:6 * d]

        def x_copy(i, s):
            return pltpu.make_async_copy(x1_ref.at[b, pl.ds(i * zr, zr), :], xbuf.at[s], xsem.at[s])

        def o_copy(i, s):
            return pltpu.make_async_copy(obuf.at[s], o_ref.at[b, pl.ds(i * zr, zr), :], osem.at[s])

        x_copy(0, 0).start()

        def fbody(i, _):
            s = i % 2
            x_copy(i, s).wait()

            @pl.when(i + 1 < nblk)
            def _():
                x_copy(i + 1, 1 - s).start()

            @pl.when(i >= 2)
            def _():
                o_copy(i - 2, s).wait()
            f = f_scr[pl.ds(i * zr, zr), :]
            fn = f * lax.rsqrt(jnp.mean(f * f, axis=-1, keepdims=True) + NORM_EPS) * npost_ref[...]
            obuf[s] = xbuf[s] + gt2 * fn
            o_copy(i, s).start()
            return 0
        lax.fori_loop(0, nblk, fbody, 0)
        o_copy(nblk - 2, nblk % 2).wait()
        o_copy(nblk - 1, (nblk - 1) % 2).wait()

    @pl.when((flat == last) & (fc == nf - 1))
    def _():
        wait_rows(1 - slot)


def _moe_call(idx_flat, h2, gate4, w_gate, w_up, w_down, x1, mod, npost, cap, tf):
    bsz, t, d = h2.shape
    ne, _, ff = w_gate.shape
    nf = ff // tf
    zr = 256
    kern = functools.partial(_moe_kernel, t=t, cap=cap, ne=ne, nf=nf, zr=zr)
    grid_spec = pltpu.PrefetchScalarGridSpec(
        num_scalar_prefetch=1,
        grid=(bsz, ne, nf),
        in_specs=[pl.BlockSpec(memory_space=pl.ANY),
                  pl.BlockSpec((1, 1, cap, 1), lambda b, e, f, idx: (b, e, 0, 0)),
                  pl.BlockSpec((1, d, tf), lambda b, e, f, idx: (e, 0, f)),
                  pl.BlockSpec((1, d, tf), lambda b, e, f, idx: (e, 0, f)),
                  pl.BlockSpec((1, tf, d), lambda b, e, f, idx: (e, f, 0)),
                  pl.BlockSpec(memory_space=pl.ANY),
                  pl.BlockSpec(mod.shape, lambda b, e, f, idx: (0, 0)),
                  pl.BlockSpec((1, d), lambda b, e, f, idx: (0, 0))],
        out_specs=pl.BlockSpec(memory_space=pl.ANY),
        scratch_shapes=[pltpu.VMEM((t, d), F32),
                        pltpu.VMEM((2, cap, d), F32),
                        pltpu.VMEM((cap, d), BF16),
                        pltpu.VMEM((cap, d), F32),
                        pltpu.VMEM((2, zr, d), F32),
                        pltpu.VMEM((2, zr, d), F32),
                        pltpu.SemaphoreType.DMA((2,)),
                        pltpu.SemaphoreType.DMA((2,)),
                        pltpu.SemaphoreType.DMA((2,))],
    )
    return pl.pallas_call(
        kern,
        grid_spec=grid_spec,
        out_shape=jax.ShapeDtypeStruct((bsz, t, d), F32),
        compiler_params=_cparams(("arbitrary", "arbitrary", "arbitrary")),
        name="moe",
    )(idx_flat, h2, gate4, w_gate, w_up, w_down, x1, mod, npost)


def _rope_tables(t, ctx_len):
    half = DA_HEAD_DIM // 2
    inv_freq = 1.0 / (ROPE_THETA ** (jnp.arange(0, half, 2, dtype=F32) / half))
    pos = jnp.arange(t, dtype=jnp.int32)
    r = (pos // GRID_W).astype(F32)
    col = (pos % GRID_W).astype(F32)
    ang_r = r[:, None] * inv_freq
    ang_c = col[:, None] * inv_freq
    ang = jnp.concatenate([ang_r, ang_r, ang_c, ang_c], axis=-1)
    cos = jnp.concatenate([jnp.cos(ang), jnp.ones((ctx_len, DA_HEAD_DIM), F32)], axis=0)
    sin = jnp.concatenate([jnp.sin(ang), jnp.zeros((ctx_len, DA_HEAD_DIM), F32)], axis=0)
    quarter = DA_HEAD_DIM // 4
    sign = jnp.where((jnp.arange(DA_HEAD_DIM) % (2 * quarter)) < quarter, -1.0, 1.0).astype(F32)
    return jnp.tile(cos, (1, 2)), jnp.tile(sin * sign, (1, 2))


def kernel(x, c, ctx, c_ctx, w_ada, b_ada, norm_pre_mix, norm_post_mix, norm_pre_ffn, norm_post_ffn, w_in,
           da_lambda_q1, da_lambda_k1, da_lambda_q2, da_lambda_k2, da_subln, hg_lower_bound, hg_norm, w_out,
           w_router, w_gate, w_up, w_down):
    bsz, t, d = x.shape
    ctx_len = ctx.shape[1]
    assert bsz <= 2 and d == 1024 and w_ada.shape[0] == 1
    cap = CAPACITY_FACTOR * t // N_EXPERTS
    lambda_init = 0.8 - 0.6 * math.exp(-0.3 * 0)

    lam = (jnp.exp(jnp.sum(da_lambda_q1[0].astype(F32) * da_lambda_k1[0].astype(F32)))
           - jnp.exp(jnp.sum(da_lambda_q2[0].astype(F32) * da_lambda_k2[0].astype(F32))) + lambda_init)
    lam = lam.reshape(1, 1)
    lb = jnp.cumsum(jax.nn.softmax(hg_lower_bound.astype(F32), axis=0), axis=0)[0]
    lb4 = lb.reshape(2, HG_HEADS, 1, HG_HEAD_DIM)

    wi = w_in[0]
    scale = DA_HEAD_DIM ** -0.5 * math.log2(math.e)
    wq = wi[:, 0:DA_WIDTH] * scale
    w_all = jnp.concatenate([wq, wi[:, DA_WIDTH:]], axis=1).astype(BF16)
    cos, sin = _rope_tables(t, ctx_len)

    cmat = jnp.concatenate([c, c_ctx[None, :], jnp.zeros((8 - bsz - 1, d), F32)], axis=0)
    mod = _mod_call(cmat, w_ada[0], b_ada[0][None, :])

    qkv, z, iqg = _inproj_call(x, ctx, mod, norm_pre_mix[0][None, :], w_all, cos, sin)

    tq = min(ATTN_TQ, t)
    nk = t + ctx_len
    tk = ATTN_TK if nk % ATTN_TK == 0 else ctx_len
    out_da = _attn_call(lam, qkv, da_subln[0][None, :], t, tq, tk, 1.0 - lambda_init)
    o_fwd = _hgrn_call(z, iqg, lb4, ctx_len, 0)
    o_bwd = _hgrn_call(z, iqg, lb4, ctx_len, 1)

    wr = w_router[0].T
    wr_hi = wr.astype(BF16)
    wr_lo = (wr - wr_hi.astype(F32)).astype(BF16)
    x1, h2, aff = _post_call(out_da, o_fwd, o_bwd, iqg, x, mod, hg_norm[0][None, :], w_out[0].astype(BF16),
                             norm_post_mix[0][None, :], norm_pre_ffn[0][None, :], wr_hi, wr_lo, POST_TM)
    idx, gate = _topk_call(aff, cap)
    return _moe_call(idx.reshape(-1), h2, gate.reshape(bsz, N_EXPERTS, cap, 1), w_gate[0], w_up[0], w_down[0],
                     x1, mod, norm_post_ffn[0][None, :], cap, 256)
```

```python
import functools
import math

import jax
import jax.numpy as jnp
import numpy as np
from jax import lax
from jax.experimental import pallas as pl
from jax.experimental.pallas import tpu as pltpu

F32 = jnp.float32
BF16 = jnp.bfloat16

GRID_W = 64
DA_HEADS = 4
DA_HEAD_DIM = 64
DA_V_DIM = 128
DA_WIDTH = 512
HG_HEADS = 4
HG_HEAD_DIM = 128
HG_WIDTH = 512
N_EXPERTS = 16
CAPACITY_FACTOR = 2
ROPE_THETA = 10000.0
NORM_EPS = 1e-6
LANES = 128
VMEM_LIMIT = 60 * 1024 * 1024
ATTN_TQ = 1024
ATTN_TK = 768
ATTN_ROW_BLOCK = 128
POST_TM = 512
SCATTER_GROUP = 8


def _cparams(sem, vmem=VMEM_LIMIT):
    return pltpu.CompilerParams(dimension_semantics=sem, vmem_limit_bytes=vmem)


def _dot(a, b):
    return jnp.dot(a, b, preferred_element_type=F32)


def _dot_nt(a, b):
    return lax.dot_general(a, b, (((1,), (1,)), ((), ())), preferred_element_type=F32)


def _dot_tn(a, b):
    return lax.dot_general(a, b, (((0,), (0,)), ((), ())), preferred_element_type=F32)


def _split3(x):
    hi = x.astype(BF16)
    r1 = x - hi.astype(F32)
    mid = r1.astype(BF16)
    lo = (r1 - mid.astype(F32)).astype(BF16)
    return hi, mid, lo


def _mod_kernel(s_ref, w_ref, b_ref, o_ref):
    s = s_ref[...]
    s = s * jax.nn.sigmoid(s)
    hi, mid, lo = _split3(s)
    w = w_ref[...]
    whi, wmid, wlo = _split3(w)
    acc = _dot(hi, whi) + (_dot(hi, wmid) + _dot(mid, whi)) + (_dot(hi, wlo) + _dot(mid, wmid) + _dot(lo, whi))
    o_ref[...] = acc + b_ref[...]


def _mod_call(cmat, w_ada, b_ada):
    d = cmat.shape[1]
    n = w_ada.shape[1]
    tn = 1536 if n % 1536 == 0 else n
    return pl.pallas_call(
        _mod_kernel,
        grid=(n // tn,),
        in_specs=[pl.BlockSpec((8, d), lambda j: (0, 0)),
                  pl.BlockSpec((d, tn), lambda j: (0, j)),
                  pl.BlockSpec((1, tn), lambda j: (0, j))],
        out_specs=pl.BlockSpec((8, tn), lambda j: (0, j)),
        out_shape=jax.ShapeDtypeStruct((8, n), F32),
        compiler_params=_cparams(("arbitrary",)),
        name="mod",
    )(cmat, w_ada, b_ada)


def _inproj_kernel(x_ref, ctx_ref, mod_ref, g_ref, w_ref, cos_ref, sin_ref, qkv_ref, z_ref, iqg_ref, *, n_lat, d):
    b = pl.program_id(0)
    i = pl.program_id(1)
    is_ctx = i == n_lat
    xb = jnp.where(is_ctx, ctx_ref[0], x_ref[0])
    row = jnp.where(is_ctx, 2, b)
    sh = mod_ref[pl.ds(row, 1), 0:d]
    sc = mod_ref[pl.ds(row, 1), d:2 * d]
    ms = jnp.mean(xb * xb, axis=-1, keepdims=True)
    h = xb * lax.rsqrt(ms + NORM_EPS) * g_ref[...]
    h = h * (1.0 + sc) + sh
    p = _dot(h.astype(BF16), w_ref[...])
    quarter = DA_HEAD_DIM // 4
    lane = lax.broadcasted_iota(jnp.int32, (p.shape[0], LANES), 1)
    first = (lane % (2 * quarter)) < quarter
    qk = []
    for j in range(2 * DA_HEADS):
        xj = p[:, j * LANES:(j + 1) * LANES]
        partner = jnp.where(first, pltpu.roll(xj, LANES - quarter, axis=1), pltpu.roll(xj, quarter, axis=1))
        qk.append(xj * cos_ref[...] + partner * sin_ref[...])
    qk = jnp.concatenate(qk, axis=-1)
    for j in range(2 * DA_HEADS):
        qkv_ref[0, j] = qk[:, j * LANES:(j + 1) * LANES].astype(BF16)
    for j in range(DA_HEADS):
        qkv_ref[0, 2 * DA_HEADS + j] = p[:, 1024 + j * LANES:1024 + (j + 1) * LANES].astype(BF16)
    z_ref[0] = p[:, 1536:2560]
    iqg_ref[0] = p[:, 2560:4096].astype(BF16)


def _inproj_call(x, ctx, mod, g, w, cos, sin):
    bsz, t, d = x.shape
    tm = ctx.shape[1]
    n_lat = t // tm
    rows = t + tm
    kern = functools.partial(_inproj_kernel, n_lat=n_lat, d=d)
    return pl.pallas_call(
        kern,
        grid=(bsz, n_lat + 1),
        in_specs=[pl.BlockSpec((1, tm, d), lambda b, i: (b, jnp.minimum(i, n_lat - 1), 0)),
                  pl.BlockSpec((1, tm, d), lambda b, i: (b, 0, 0)),
                  pl.BlockSpec(mod.shape, lambda b, i: (0, 0)),
                  pl.BlockSpec((1, d), lambda b, i: (0, 0)),
                  pl.BlockSpec(w.shape, lambda b, i: (0, 0)),
                  pl.BlockSpec((tm, LANES), lambda b, i: (i, 0)),
                  pl.BlockSpec((tm, LANES), lambda b, i: (i, 0))],
        out_specs=[pl.BlockSpec((1, 3 * DA_HEADS, tm, LANES), lambda b, i: (b, 0, i, 0)),
                   pl.BlockSpec((1, tm, 1024), lambda b, i: (b, i, 0)),
                   pl.BlockSpec((1, tm, 1536), lambda b, i: (b, i, 0))],
        out_shape=[jax.ShapeDtypeStruct((bsz, 3 * DA_HEADS, rows, LANES), BF16),
                   jax.ShapeDtypeStruct((bsz, rows, 1024), F32),
                   jax.ShapeDtypeStruct((bsz, rows, 1536), BF16)],
        compiler_params=_cparams(("parallel", "arbitrary")),
        name="inproj",
    )(x, ctx, mod, g, w, cos, sin)


def _attn_kernel(lam_ref, q_ref, k_ref, v_ref, subln_ref, o_ref, qs_ref, sa_ref, sb_ref, pa_ref, pb_ref, m_ref,
                 acc_ref, *, tq, tk, nchunk, out_scale):
    q = q_ref[0, 0]
    lane = lax.broadcasted_iota(jnp.int32, q.shape, 1)
    zero = jnp.zeros_like(q)
    qs_ref[0:tq, :] = jnp.where(lane < DA_HEAD_DIM, q, zero)
    qs_ref[tq:2 * tq, :] = jnp.where(lane >= DA_HEAD_DIM, q, zero)

    m_ref[...] = jnp.full(m_ref.shape, -1e30, F32)
    acc_ref[...] = jnp.zeros(acc_ref.shape, F32)

    def scores(j, s_ref):
        kj = k_ref[0, 0, pl.ds(pl.multiple_of(j * tk, tk), tk), :]
        s_ref[...] = _dot_nt(qs_ref[...], kj)

    def update(j, s_ref, p_ref):
        vj = v_ref[0, 0, pl.ds(pl.multiple_of(j * tk, tk), tk), :]
        vext = jnp.concatenate([vj, jnp.ones_like(vj)], axis=-1)
        m = m_ref[...]
        m_new = jnp.maximum(m, jnp.broadcast_to(jnp.max(s_ref[...], axis=-1, keepdims=True), m.shape))
        alpha = jnp.exp2(m - m_new)
        m_ref[...] = m_new
        rb = min(ATTN_ROW_BLOCK, 2 * tq)
        for r0 in range(0, 2 * tq, rb):
            rows = slice(r0, r0 + rb)
            mb = m_ref[rows, :]
            for c0 in range(0, tk, LANES):
                p_ref[rows, c0:c0 + LANES] = jnp.exp2((s_ref[rows, c0:c0 + LANES] - mb).astype(BF16))
        acc_ref[...] = jnp.tile(alpha, (1, 2)) * acc_ref[...] + _dot(p_ref[...], vext)

    npairs = (nchunk - 1) // 2
    scores(0, sa_ref)

    def body(i, _):
        scores(2 * i + 1, sb_ref)
        update(2 * i, sa_ref, pa_ref)
        scores(2 * i + 2, sa_ref)
        update(2 * i + 1, sb_ref, pb_ref)
        return 0
    lax.fori_loop(0, npairs, body, 0)
    if nchunk - 1 == 2 * npairs:
        update(nchunk - 1, sa_ref, pa_ref)
    else:
        scores(nchunk - 1, sb_ref)
        update(nchunk - 2, sa_ref, pa_ref)
        update(nchunk - 1, sb_ref, pb_ref)

    r = 1.0 / acc_ref[:, LANES:LANES + 1]
    acc = acc_ref[:, 0:LANES]
    o = acc[0:tq] * r[0:tq] - lam_ref[0, 0] * (acc[tq:2 * tq] * r[tq:2 * tq])
    ms = jnp.mean(o * o, axis=-1, keepdims=True)
    o = o * lax.rsqrt(ms + NORM_EPS) * subln_ref[...] * out_scale
    o_ref[0] = o.astype(o_ref.dtype)


def _attn_call(lam, qkv, subln, t, tq, tk, out_scale):
    bsz, _, rows, _ = qkv.shape
    kern = functools.partial(_attn_kernel, tq=tq, tk=tk, nchunk=rows // tk, out_scale=out_scale)
    return pl.pallas_call(
        kern,
        grid=(bsz, DA_HEADS, t // tq),
        in_specs=[pl.BlockSpec(memory_space=pltpu.SMEM),
                  pl.BlockSpec((1, 1, tq, LANES), lambda b, h, i: (b, h, i, 0)),
                  pl.BlockSpec((1, 1, rows, LANES), lambda b, h, i: (b, DA_HEADS + h, 0, 0)),
                  pl.BlockSpec((1, 1, rows, LANES), lambda b, h, i: (b, 2 * DA_HEADS + h, 0, 0)),
                  pl.BlockSpec((1, LANES), lambda b, h, i: (0, 0))],
        out_specs=pl.BlockSpec((1, tq, LANES), lambda b, h, i: (b, i, h)),
        out_shape=jax.ShapeDtypeStruct((bsz, t, DA_WIDTH), BF16),
        scratch_shapes=[pltpu.VMEM((2 * tq, LANES), BF16),
                        pltpu.VMEM((2 * tq, tk), F32),
                        pltpu.VMEM((2 * tq, tk), F32),
                        pltpu.VMEM((2 * tq, tk), BF16),
                        pltpu.VMEM((2 * tq, tk), BF16),
                        pltpu.VMEM((2 * tq, LANES), F32),
                        pltpu.VMEM((2 * tq, 2 * LANES), F32)],
        compiler_params=_cparams(("parallel", "parallel", "arbitrary")),
        name="attn",
    )(lam, qkv, qkv, qkv, subln)


HGRN_SUB_LEVELS = 3


def _hgrn_consts(c, d):
    nl = int(math.log2(c))
    t = np.arange(c)
    pi = t if d == 0 else c - 1 - t
    pr, pu = pi[:, None], pi[None, :]
    lm = np.zeros(((HGRN_SUB_LEVELS + 1) * c, c), np.float32)
    for lg in range(HGRN_SUB_LEVELS):
        same = (pr >> lg) == (pu >> lg)
        sec = ((pr >> lg) & 1) == 1
        lm[lg * c:(lg + 1) * c] = same & np.where(sec, pu <= pr, pu > pr)
    lm[HGRN_SUB_LEVELS * c:] = pu <= pr
    masks = np.zeros((nl + 1, c, c), np.float32)
    for lg in range(nl):
        masks[lg] = (t[:, None] >> (lg + 1)) == (t[None, :] >> (lg + 1))
    masks[nl] = np.eye(c)
    return jnp.asarray(lm, BF16), jnp.asarray(masks, F32), nl


def _hgrn_kernel(z_ref, i_ref, q_ref, lb_ref, lmat_ref, mask_ref, o_ref, st_ref, *, c, nl, d):
    s = pl.program_id(1)
    heads = range(HG_HEADS)

    @pl.when(s == 0)
    def _():
        st_ref[...] = jnp.zeros(st_ref.shape, F32)

    def col(h):
        return slice(h * LANES, (h + 1) * LANES)

    f = [lb_ref[0, h] + (1.0 - lb_ref[0, h]) * jax.nn.sigmoid(z_ref[0, :, col(h)]) for h in heads]
    lf = [jnp.log(f[h]) for h in heads]
    kk = [1.0 - f[h] for h in heads]
    v = [i_ref[0, :, col(h)] for h in heads]
    q = [q_ref[0, :, col(h)].astype(F32) for h in heads]
    x = []
    for h in heads:
        hi = lf[h].astype(BF16)
        lo = (lf[h] - hi.astype(F32)).astype(BF16)
        x2 = _dot(lmat_ref[...], jnp.concatenate([hi, lo], axis=-1))
        x.append(x2[:, 0:LANES] + x2[:, LANES:2 * LANES])
    cum = [x[h][HGRN_SUB_LEVELS * c:] for h in heads]
    last = c - 1 if d == 0 else 0
    bend = [cum[h][last:last + 1] for h in heads]

    row = lax.broadcasted_iota(jnp.int32, (c, LANES), 0)
    scores = [mask_ref[nl] * _dot_nt(q[h].astype(BF16), kk[h].astype(BF16)) for h in heads]
    for lg in range(nl):
        hs = 1 << lg
        sec = ((row >> lg) & 1) == (1 - d)
        for h in heads:
            if lg < HGRN_SUB_LEVELS:
                xl = x[h][lg * c:(lg + 1) * c]
            else:
                mid = hs - 1 if d == 0 else hs
                c3 = cum[h].reshape(c // (2 * hs), 2 * hs, LANES)
                ref = jnp.broadcast_to(c3[:, mid:mid + 1, :], c3.shape).reshape(c, LANES)
                xl = -jnp.abs(cum[h] - ref)
            g = jnp.exp(xl)
            qt = jnp.where(sec, q[h] * g, 0.0).astype(BF16)
            kt = jnp.where(sec, 0.0, kk[h] * g).astype(BF16)
            prod = _dot_nt(qt, kt)
            scores[h] = scores[h] + (prod if lg == nl - 1 else mask_ref[lg] * prod)
    for h in heads:
        st = st_ref[h]
        o = (_dot_nt((q[h] * jnp.exp(cum[h])).astype(BF16), st.astype(BF16))
             + _dot(scores[h].astype(BF16), v[h]))
        kb = (kk[h] * jnp.exp(bend[h] - cum[h])).astype(BF16)
        st_ref[h] = st * jnp.exp(bend[h]) + _dot_tn(v[h], kb)
        o_ref[0, :, col(h)] = o.astype(o_ref.dtype)


def _hgrn_call(z, iqg, lb4, c, d):
    bsz, rows, _ = z.shape
    nb = rows // c - 1
    lmat, masks, nl = _hgrn_consts(c, d)

    def blk(s):
        return jnp.where(s == 0, nb, s - 1 if d == 0 else nb - s)

    kern = functools.partial(_hgrn_kernel, c=c, nl=nl, d=d)
    return pl.pallas_call(
        kern,
        grid=(bsz, nb + 1),
        in_specs=[pl.BlockSpec((1, c, HG_WIDTH), lambda b, s: (b, blk(s), d)),
                  pl.BlockSpec((1, c, HG_WIDTH), lambda b, s: (b, blk(s), 0)),
                  pl.BlockSpec((1, c, HG_WIDTH), lambda b, s: (b, blk(s), 1)),
                  pl.BlockSpec((1, HG_HEADS, 1, LANES), lambda b, s: (d, 0, 0, 0)),
                  pl.BlockSpec(lmat.shape, lambda b, s: (0, 0)),
                  pl.BlockSpec(masks.shape, lambda b, s: (0, 0, 0))],
        out_specs=pl.BlockSpec((1, c, HG_WIDTH), lambda b, s: (b, blk(s), 0)),
        out_shape=jax.ShapeDtypeStruct((bsz, rows, HG_WIDTH), BF16),
        scratch_shapes=[pltpu.VMEM((HG_HEADS, HG_HEAD_DIM, HG_HEAD_DIM), F32)],
        compiler_params=_cparams(("parallel", "arbitrary")),
        name="hgrn_bwd" if d else "hgrn_fwd",
    )(z, iqg, iqg, lb4, lmat, masks)


def _post_kernel(da_ref, of_ref, ob_ref, g_ref, x_ref, mod_ref, hgn_ref, wout_ref, npost_ref, npre_ref,
                 wrh_ref, wrl_ref, x1_ref, h2_ref, aff_ref, *, d):
    b = pl.program_id(0)
    o = of_ref[0].astype(F32) + ob_ref[0].astype(F32)
    parts = []
    for h in range(HG_HEADS):
        oh = o[:, h * LANES:(h + 1) * LANES]
        ms = jnp.mean(oh * oh, axis=-1, keepdims=True)
        parts.append(oh * lax.rsqrt(ms + NORM_EPS) * hgn_ref[...])
    g = g_ref[0].astype(F32)
    ohn = jnp.concatenate(parts, axis=-1) * (g * jax.nn.sigmoid(g))
    mixed = jnp.concatenate([da_ref[0], ohn.astype(BF16)], axis=-1)
    y = _dot(mixed, wout_ref[...])
    yn = y * lax.rsqrt(jnp.mean(y * y, axis=-1, keepdims=True) + NORM_EPS) * npost_ref[...]
    gt1 = mod_ref[pl.ds(b, 1), 2 * d:3 * d]
    sh2 = mod_ref[pl.ds(b, 1), 3 * d:4 * d]
    sc2 = mod_ref[pl.ds(b, 1), 4 * d:5 * d]
    x1 = x_ref[0] + gt1 * yn
    x1_ref[0] = x1
    h2 = x1 * lax.rsqrt(jnp.mean(x1 * x1, axis=-1, keepdims=True) + NORM_EPS) * npre_ref[...]
    h2 = h2 * (1.0 + sc2) + sh2
    h2_ref[0] = h2
    hh = h2.astype(BF16)
    hl = (h2 - hh.astype(F32)).astype(BF16)
    logits = _dot_nt(wrh_ref[...], hh) + (_dot_nt(wrh_ref[...], hl) + _dot_nt(wrl_ref[...], hh))
    mx = jnp.max(logits, axis=0, keepdims=True)
    e = jnp.exp(logits - mx)
    aff_ref[0] = e / jnp.sum(e, axis=0, keepdims=True)


def _post_call(out_da, o_fwd, o_bwd, iqg, x, mod, hg_norm, w_out, npost, npre, wr_hi, wr_lo, tm):
    bsz, t, d = x.shape
    kern = functools.partial(_post_kernel, d=d)
    row = lambda b, i: (0, 0)
    return pl.pallas_call(
        kern,
        grid=(bsz, t // tm),
        in_specs=[pl.BlockSpec((1, tm, DA_WIDTH), lambda b, i: (b, i, 0)),
                  pl.BlockSpec((1, tm, HG_WIDTH), lambda b, i: (b, i, 0)),
                  pl.BlockSpec((1, tm, HG_WIDTH), lambda b, i: (b, i, 0)),
                  pl.BlockSpec((1, tm, HG_WIDTH), lambda b, i: (b, i, 2)),
                  pl.BlockSpec((1, tm, d), lambda b, i: (b, i, 0)),
                  pl.BlockSpec(mod.shape, row),
                  pl.BlockSpec((1, LANES), row),
                  pl.BlockSpec(w_out.shape, row),
                  pl.BlockSpec((1, d), row),
                  pl.BlockSpec((1, d), row),
                  pl.BlockSpec(wr_hi.shape, row),
                  pl.BlockSpec(wr_lo.shape, row)],
        out_specs=[pl.BlockSpec((1, tm, d), lambda b, i: (b, i, 0)),
                   pl.BlockSpec((1, tm, d), lambda b, i: (b, i, 0)),
                   pl.BlockSpec((1, N_EXPERTS, tm), lambda b, i: (b, 0, i))],
        out_shape=[jax.ShapeDtypeStruct((bsz, t, d), F32),
                   jax.ShapeDtypeStruct((bsz, t, d), F32),
                   jax.ShapeDtypeStruct((bsz, N_EXPERTS, t), F32)],
        compiler_params=_cparams(("parallel", "arbitrary")),
        name="post",
    )(out_da, o_fwd, o_bwd, iqg, x, mod, hg_norm, w_out, npost, npre, wr_hi, wr_lo)


TOPK_BLOCK = 256
TOPK_COUNT_BLOCK = 1024


def _topk_kernel(aff_ref, utri_ref, idx_ref, gate_ref, selc_ref, cnt_ref, cnt_smem, acci_ref, accg_ref,
                 *, t, cap):
    ne = N_EXPERTS
    cb = TOPK_BLOCK
    nblk = t // cb
    kb = min(TOPK_COUNT_BLOCK, t)

    def as_float(bits):
        return lax.bitcast_convert_type(bits, F32)

    def count_ge(thr):
        thr_b = jnp.broadcast_to(thr, (ne, kb))

        def body(i, acc):
            a = aff_ref[0, :, pl.ds(pl.multiple_of(i * kb, kb), kb)]
            return acc + jnp.where(a >= thr_b, 1, 0)
        acc = lax.fori_loop(0, t // kb, body, jnp.zeros((ne, kb), jnp.int32))
        return jnp.sum(acc, axis=1, keepdims=True)

    def bit_body(k, tau):
        cand = tau | (jnp.int32(1) << (29 - k))
        return jnp.where(count_ge(as_float(cand)) >= cap, cand, tau)
    tau = lax.fori_loop(0, 30, bit_body, jnp.zeros((ne, 1), jnp.int32))
    tau_f = as_float(tau)
    nxt_f = as_float(tau + 1)

    n_gt = count_ge(nxt_f)
    need = (cap - n_gt).astype(F32)

    utri = utri_ref[...]

    def sel_body(i, carry):
        ceq, csel = carry
        a = aff_ref[0, :, pl.ds(pl.multiple_of(i * cb, cb), cb)]
        gt = a >= nxt_f
        eq = (a >= tau_f) & jnp.logical_not(gt)
        eq_rank = _dot(jnp.where(eq, 1.0, 0.0).astype(BF16), utri) + ceq
        sel = gt | (eq & (eq_rank <= need))
        c_incl = _dot(jnp.where(sel, 1.0, 0.0).astype(BF16), utri) + csel
        selc_ref[:, pl.ds(pl.multiple_of(i * cb, cb), cb)] = jnp.where(sel, c_incl, 0.0)
        cnt_ref[i] = jnp.broadcast_to(csel, (ne, LANES)).astype(jnp.int32)
        return eq_rank[:, cb - 1:cb], c_incl[:, cb - 1:cb]
    zero = jnp.zeros((ne, 1), F32)
    _, c_all = lax.fori_loop(0, nblk, sel_body, (zero, zero))
    cnt_ref[nblk] = jnp.broadcast_to(c_all, (ne, LANES)).astype(jnp.int32)

    def to_smem(i, _):
        blk = cnt_ref[i]
        for e in range(ne):
            cnt_smem[i * ne + e] = blk[e, 0]
        return 0
    lax.fori_loop(0, nblk + 1, to_smem, 0)

    n_pt = cap // LANES
    slot = lax.broadcasted_iota(jnp.int32, (LANES, cb), 0)
    for e in range(ne):
        acci_ref[...] = jnp.zeros(acci_ref.shape, F32)
        accg_ref[...] = jnp.zeros(accg_ref.shape, F32)

        def slot_body(i, _):
            cols = pl.ds(pl.multiple_of(i * cb, cb), cb)
            cc = jnp.broadcast_to(selc_ref[e:e + 1, cols], (LANES, cb))
            aa = jnp.broadcast_to(aff_ref[0, e:e + 1, cols], (LANES, cb))
            tt = (lax.broadcasted_iota(jnp.int32, (LANES, cb), 1) + i * cb).astype(F32)
            first = cnt_smem[i * ne + e]
            last = cnt_smem[(i + 1) * ne + e] - 1

            def tile_body(j, _):
                hit = cc == (slot + (j * LANES + 1)).astype(F32)
                ti = jnp.where(hit, tt, 0.0)
                gi = jnp.where(hit, aa, 0.0)
                for c0 in range(0, cb, LANES):
                    acci_ref[j] = acci_ref[j] + ti[:, c0:c0 + LANES]
                    accg_ref[j] = accg_ref[j] + gi[:, c0:c0 + LANES]
                return 0
            lax.fori_loop(first // LANES, last // LANES + 1, tile_body, 0)
            return 0
        lax.fori_loop(0, nblk, slot_body, 0)
        for j in range(n_pt):
            idx_ref[0, e:e + 1, j * LANES:(j + 1) * LANES] = (
                jnp.sum(acci_ref[j].T, axis=0, keepdims=True).astype(jnp.int32))
            gate_ref[0, e:e + 1, j * LANES:(j + 1) * LANES] = jnp.sum(accg_ref[j].T, axis=0, keepdims=True)


def _topk_call(aff, cap):
    bsz, ne, t = aff.shape
    cb = TOPK_BLOCK
    utri = jnp.asarray(np.triu(np.ones((cb, cb), np.float32)), BF16)
    kern = functools.partial(_topk_kernel, t=t, cap=cap)
    return pl.pallas_call(
        kern,
        grid=(bsz,),
        in_specs=[pl.BlockSpec((1, ne, t), lambda b: (b, 0, 0)),
                  pl.BlockSpec((cb, cb), lambda b: (0, 0))],
        out_specs=[pl.BlockSpec((1, ne, cap), lambda b: (b, 0, 0)),
                   pl.BlockSpec((1, ne, cap), lambda b: (b, 0, 0))],
        out_shape=[jax.ShapeDtypeStruct((bsz, ne, cap), jnp.int32),
                   jax.ShapeDtypeStruct((bsz, ne, cap), F32)],
        scratch_shapes=[pltpu.VMEM((ne, t), F32),
                        pltpu.VMEM((t // cb + 1, ne, LANES), jnp.int32),
                        pltpu.SMEM(((t // cb + 1) * ne,), jnp.int32),
                        pltpu.VMEM((cap // LANES, LANES, LANES), F32),
                        pltpu.VMEM((cap // LANES, LANES, LANES), F32)],
        compiler_params=_cparams(("arbitrary",)),
        name="topk",
    )(aff, utri)


def _moe_kernel(idx_ref, h2_ref, gate_ref, wg_ref, wu_ref, wd_ref, x1_ref, mod_ref, npost_ref, o_ref,
                f_scr, xin, xbf, yacc, xbuf, obuf, gsem, xsem, osem, *, t, cap, ne, nf, zr):
    b = pl.program_id(0)
    e = pl.program_id(1)
    fc = pl.program_id(2)
    nb = pl.num_programs(0)
    flat = b * ne + e
    base = flat * cap
    slot = flat % 2
    last = nb * ne - 1
    nxt = jnp.minimum(flat + 1, last)
    rows_per_fc = cap // nf

    @pl.when((e == 0) & (fc == 0))
    def _():
        def zbody(i, _):
            f_scr[pl.ds(i * zr, zr), :] = jnp.zeros((zr, f_scr.shape[1]), F32)
            return 0
        lax.fori_loop(0, t // zr, zbody, 0)
        yacc[...] = jnp.zeros(yacc.shape, F32)

    def row_copy(step, dst, p):
        tok = idx_ref[step * cap + p]
        return pltpu.make_async_copy(h2_ref.at[step // ne, pl.ds(tok, 1), :],
                                     xin.at[dst, pl.ds(p, 1), :], gsem.at[dst])

    def wait_rows(dst):
        pltpu.make_async_copy(h2_ref.at[0, pl.ds(0, cap), :], xin.at[dst], gsem.at[dst]).wait()

    @pl.when((flat == 0) & (fc == 0))
    def _():
        def gstart(p, _):
            row_copy(flat, slot, p).start()
            return 0
        lax.fori_loop(0, cap, gstart, 0, unroll=8)

    @pl.when(fc == 0)
    def _():
        wait_rows(slot)
        xbf[...] = xin[slot].astype(BF16)

    for p in range(rows_per_fc):
        row_copy(nxt, 1 - slot, fc * rows_per_fc + p).start()

    wgb = wg_ref[0].astype(BF16)
    wub = wu_ref[0].astype(BF16)
    wdb = wd_ref[0].astype(BF16)
    half = cap // 2
    first = fc == 0

    def gate_up(rows):
        x = xbf[rows, :]
        return _dot(x, wgb), _dot(x, wub)

    def swiglu(gu):
        g, u = gu
        return (g * jax.nn.sigmoid(g) * u).astype(BF16)

    def down(rows, hmid):
        contrib = _dot(hmid, wdb) * gate_ref[0, 0, rows, :]
        yacc[rows, :] = jnp.where(first, contrib, yacc[rows, :] + contrib)

    rows_a, rows_b = slice(0, half), slice(half, cap)
    gu_a = gate_up(rows_a)
    h_a = swiglu(gu_a)
    gu_b = gate_up(rows_b)
    down(rows_a, h_a)
    h_b = swiglu(gu_b)
    down(rows_b, h_b)

    @pl.when(fc == nf - 1)
    def _():
        def sbody(g, _):
            p0 = g * SCATTER_GROUP
            toks = [idx_ref[base + p0 + i] for i in range(SCATTER_GROUP)]
            yr = [yacc[pl.ds(p0 + i, 1), :] for i in range(SCATTER_GROUP)]
            fr = [f_scr[pl.ds(toks[i], 1), :] for i in range(SCATTER_GROUP)]
            for i in range(SCATTER_GROUP):
                f_scr[pl.ds(toks[i], 1), :] = fr[i] + yr[i]
            return 0
        lax.fori_loop(0, cap // SCATTER_GROUP, sbody, 0)

    @pl.when((e == ne - 1) & (fc == nf - 1))
    def _():
        d = f_scr.shape[1]
        nblk = t // zr
        gt2 = mod_ref[pl.ds(b, 1), 5 * d:6 * d]

        def x_copy(i, s):
            return pltpu.make_async_copy(x1_ref.at[b, pl.ds(i * zr, zr), :], xbuf.at[s], xsem.at[s])

        def o_copy(i, s):
            return pltpu.make_async_copy(obuf.at[s], o_ref.at[b, pl.ds(i * zr, zr), :], osem.at[s])

        x_copy(0, 0).start()

        def fbody(i, _):
            s = i % 2
            x_copy(i, s).wait()

            @pl.when(i + 1 < nblk)
            def _():
                x_copy(i + 1, 1 - s).start()

            @pl.when(i >= 2)
            def _():
                o_copy(i - 2, s).wait()
            f = f_scr[pl.ds(i * zr, zr), :]
            fn = f * lax.rsqrt(jnp.mean(f * f, axis=-1, keepdims=True) + NORM_EPS) * npost_ref[...]
            obuf[s] = xbuf[s] + gt2 * fn
            o_copy(i, s).start()
            return 0
        lax.fori_loop(0, nblk, fbody, 0)
        o_copy(nblk - 2, nblk % 2).wait()
        o_copy(nblk - 1, (nblk - 1) % 2).wait()

    @pl.when((flat == last) & (fc == nf - 1))
    def _():
        wait_rows(1 - slot)


def _moe_call(idx_flat, h2, gate4, w_gate, w_up, w_down, x1, mod, npost, cap, tf):
    bsz, t, d = h2.shape
    ne, _, ff = w_gate.shape
    nf = ff // tf
    zr = 256
    kern = functools.partial(_moe_kernel, t=t, cap=cap, ne=ne, nf=nf, zr=zr)
    grid_spec = pltpu.PrefetchScalarGridSpec(
        num_scalar_prefetch=1,
        grid=(bsz, ne, nf),
        in_specs=[pl.BlockSpec(memory_space=pl.ANY),
                  pl.BlockSpec((1, 1, cap, 1), lambda b, e, f, idx: (b, e, 0, 0)),
                  pl.BlockSpec((1, d, tf), lambda b, e, f, idx: (e, 0, f)),
                  pl.BlockSpec((1, d, tf), lambda b, e, f, idx: (e, 0, f)),
                  pl.BlockSpec((1, tf, d), lambda b, e, f, idx: (e, f, 0)),
                  pl.BlockSpec(memory_space=pl.ANY),
                  pl.BlockSpec(mod.shape, lambda b, e, f, idx: (0, 0)),
                  pl.BlockSpec((1, d), lambda b, e, f, idx: (0, 0))],
        out_specs=pl.BlockSpec(memory_space=pl.ANY),
        scratch_shapes=[pltpu.VMEM((t, d), F32),
                        pltpu.VMEM((2, cap, d), F32),
                        pltpu.VMEM((cap, d), BF16),
                        pltpu.VMEM((cap, d), F32),
                        pltpu.VMEM((2, zr, d), F32),
                        pltpu.VMEM((2, zr, d), F32),
                        pltpu.SemaphoreType.DMA((2,)),
                        pltpu.SemaphoreType.DMA((2,)),
                        pltpu.SemaphoreType.DMA((2,))],
    )
    return pl.pallas_call(
        kern,
        grid_spec=grid_spec,
        out_shape=jax.ShapeDtypeStruct((bsz, t, d), F32),
        compiler_params=_cparams(("arbitrary", "arbitrary", "arbitrary")),
        name="moe",
    )(idx_flat, h2, gate4, w_gate, w_up, w_down, x1, mod, npost)


def _rope_tables(t, ctx_len):
    half = DA_HEAD_DIM // 2
    inv_freq = 1.0 / (ROPE_THETA ** (jnp.arange(0, half, 2, dtype=F32) / half))
    pos = jnp.arange(t, dtype=jnp.int32)
    r = (pos // GRID_W).astype(F32)
    col = (pos % GRID_W).astype(F32)
    ang_r = r[:, None] * inv_freq
    ang_c = col[:, None] * inv_freq
    ang = jnp.concatenate([ang_r, ang_r, ang_c, ang_c], axis=-1)
    cos = jnp.concatenate([jnp.cos(ang), jnp.ones((ctx_len, DA_HEAD_DIM), F32)], axis=0)
    sin = jnp.concatenate([jnp.sin(ang), jnp.zeros((ctx_len, DA_HEAD_DIM), F32)], axis=0)
    quarter = DA_HEAD_DIM // 4
    sign = jnp.where((jnp.arange(DA_HEAD_DIM) % (2 * quarter)) < quarter, -1.0, 1.0).astype(F32)
    return jnp.tile(cos, (1, 2)), jnp.tile(sin * sign, (1, 2))


def kernel(x, c, ctx, c_ctx, w_ada, b_ada, norm_pre_mix, norm_post_mix, norm_pre_ffn, norm_post_ffn, w_in,
           da_lambda_q1, da_lambda_k1, da_lambda_q2, da_lambda_k2, da_subln, hg_lower_bound, hg_norm, w_out,
           w_router, w_gate, w_up, w_down):
    bsz, t, d = x.shape
    ctx_len = ctx.shape[1]
    assert bsz <= 2 and d == 1024 and w_ada.shape[0] == 1
    cap = CAPACITY_FACTOR * t // N_EXPERTS
    lambda_init = 0.8 - 0.6 * math.exp(-0.3 * 0)

    lam = (jnp.exp(jnp.sum(da_lambda_q1[0].astype(F32) * da_lambda_k1[0].astype(F32)))
           - jnp.exp(jnp.sum(da_lambda_q2[0].astype(F32) * da_lambda_k2[0].astype(F32))) + lambda_init)
    lam = lam.reshape(1, 1)
    lb = jnp.cumsum(jax.nn.softmax(hg_lower_bound.astype(F32), axis=0), axis=0)[0]
    lb4 = lb.reshape(2, HG_HEADS, 1, HG_HEAD_DIM)

    wi = w_in[0]
    scale = DA_HEAD_DIM ** -0.5 * math.log2(math.e)
    wq = wi[:, 0:DA_WIDTH] * scale
    w_all = jnp.concatenate([wq, wi[:, DA_WIDTH:]], axis=1).astype(BF16)
    cos, sin = _rope_tables(t, ctx_len)

    cmat = jnp.concatenate([c, c_ctx[None, :], jnp.zeros((8 - bsz - 1, d), F32)], axis=0)
    mod = _mod_call(cmat, w_ada[0], b_ada[0][None, :])

    qkv, z, iqg = _inproj_call(x, ctx, mod, norm_pre_mix[0][None, :], w_all, cos, sin)

    tq = min(ATTN_TQ, t)
    nk = t + ctx_len
    tk = ATTN_TK if nk % ATTN_TK == 0 else ctx_len
    out_da = _attn_call(lam, qkv, da_subln[0][None, :], t, tq, tk, 1.0 - lambda_init)
    o_fwd = _hgrn_call(z, iqg, lb4, ctx_len, 0)
    o_bwd = _hgrn_call(z, iqg, lb4, ctx_len, 1)

    wr = w_router[0].T
    wr_hi = wr.astype(BF16)
    wr_lo = (wr - wr_hi.astype(F32)).astype(BF16)
    x1, h2, aff = _post_call(out_da, o_fwd, o_bwd, iqg, x, mod, hg_norm[0][None, :], w_out[0].astype(BF16),
                             norm_post_mix[0][None, :], norm_pre_ffn[0][None, :], wr_hi, wr_lo, POST_TM)
    idx, gate = _topk_call(aff, cap)
    return _moe_call(idx.reshape(-1), h2, gate.reshape(bsz, N_EXPERTS, cap, 1), w_gate[0], w_up[0], w_down[0],
                     x1, mod, norm_post_ffn[0][None, :], cap, 256)
```

```python
import functools
import math

import jax
import jax.numpy as jnp
import numpy as np
from jax import lax
from jax.experimental import pallas as pl
from jax.experimental.pallas import tpu as pltpu

F32 = jnp.float32
BF16 = jnp.bfloat16

GRID_W = 64
DA_HEADS = 4
DA_HEAD_DIM = 64
DA_V_DIM = 128
DA_WIDTH = 512
HG_HEADS = 4
HG_HEAD_DIM = 128
HG_WIDTH = 512
N_EXPERTS = 16
CAPACITY_FACTOR = 2
ROPE_THETA = 10000.0
NORM_EPS = 1e-6
LANES = 128
VMEM_LIMIT = 60 * 1024 * 1024
ATTN_TQ = 1024
ATTN_TK = 768
ATTN_ROW_BLOCK = 128
POST_TM = 512
SCATTER_GROUP = 8


def _cparams(sem, vmem=VMEM_LIMIT):
    return pltpu.CompilerParams(dimension_semantics=sem, vmem_limit_bytes=vmem)


def _dot(a, b):
    return jnp.dot(a, b, preferred_element_type=F32)


def _dot_nt(a, b):
    return lax.dot_general(a, b, (((1,), (1,)), ((), ())), preferred_element_type=F32)


def _dot_tn(a, b):
    return lax.dot_general(a, b, (((0,), (0,)), ((), ())), preferred_element_type=F32)


def _split3(x):
    hi = x.astype(BF16)
    r1 = x - hi.astype(F32)
    mid = r1.astype(BF16)
    lo = (r1 - mid.astype(F32)).astype(BF16)
    return hi, mid, lo


def _mod_kernel(s_ref, w_ref, b_ref, o_ref):
    s = s_ref[...]
    s = s * jax.nn.sigmoid(s)
    hi, mid, lo = _split3(s)
    w = w_ref[...]
    whi, wmid, wlo = _split3(w)
    acc = _dot(hi, whi) + (_dot(hi, wmid) + _dot(mid, whi)) + (_dot(hi, wlo) + _dot(mid, wmid) + _dot(lo, whi))
    o_ref[...] = acc + b_ref[...]


def _mod_call(cmat, w_ada, b_ada):
    d = cmat.shape[1]
    n = w_ada.shape[1]
    tn = 1536 if n % 1536 == 0 else n
    return pl.pallas_call(
        _mod_kernel,
        grid=(n // tn,),
        in_specs=[pl.BlockSpec((8, d), lambda j: (0, 0)),
                  pl.BlockSpec((d, tn), lambda j: (0, j)),
                  pl.BlockSpec((1, tn), lambda j: (0, j))],
        out_specs=pl.BlockSpec((8, tn), lambda j: (0, j)),
        out_shape=jax.ShapeDtypeStruct((8, n), F32),
        compiler_params=_cparams(("arbitrary",)),
        name="mod",
    )(cmat, w_ada, b_ada)


def _inproj_kernel(x_ref, ctx_ref, mod_ref, g_ref, w_ref, cos_ref, sin_ref, qkv_ref, z_ref, iqg_ref, *, n_lat, d):
    b = pl.program_id(0)
    i = pl.program_id(1)
    is_ctx = i == n_lat
    xb = jnp.where(is_ctx, ctx_ref[0], x_ref[0])
    row = jnp.where(is_ctx, 2, b)
    sh = mod_ref[pl.ds(row, 1), 0:d]
    sc = mod_ref[pl.ds(row, 1), d:2 * d]
    ms = jnp.mean(xb * xb, axis=-1, keepdims=True)
    h = xb * lax.rsqrt(ms + NORM_EPS) * g_ref[...]
    h = h * (1.0 + sc) + sh
    p = _dot(h.astype(BF16), w_ref[...])
    quarter = DA_HEAD_DIM // 4
    lane = lax.broadcasted_iota(jnp.int32, (p.shape[0], LANES), 1)
    first = (lane % (2 * quarter)) < quarter
    qk = []
    for j in range(2 * DA_HEADS):
        xj = p[:, j * LANES:(j + 1) * LANES]
        partner = jnp.where(first, pltpu.roll(xj, LANES - quarter, axis=1), pltpu.roll(xj, quarter, axis=1))
        qk.append(xj * cos_ref[...] + partner * sin_ref[...])
    qk = jnp.concatenate(qk, axis=-1)
    for j in range(2 * DA_HEADS):
        qkv_ref[0, j] = qk[:, j * LANES:(j + 1) * LANES].astype(BF16)
    for j in range(DA_HEADS):
        qkv_ref[0, 2 * DA_HEADS + j] = p[:, 1024 + j * LANES:1024 + (j + 1) * LANES].astype(BF16)
    z_ref[0] = p[:, 1536:2560]
    iqg_ref[0] = p[:, 2560:4096].astype(BF16)


def _inproj_call(x, ctx, mod, g, w, cos, sin):
    bsz, t, d = x.shape
    tm = ctx.shape[1]
    n_lat = t // tm
    rows = t + tm
    kern = functools.partial(_inproj_kernel, n_lat=n_lat, d=d)
    return pl.pallas_call(
        kern,
        grid=(bsz, n_lat + 1),
        in_specs=[pl.BlockSpec((1, tm, d), lambda b, i: (b, jnp.minimum(i, n_lat - 1), 0)),
                  pl.BlockSpec((1, tm, d), lambda b, i: (b, 0, 0)),
                  pl.BlockSpec(mod.shape, lambda b, i: (0, 0)),
                  pl.BlockSpec((1, d), lambda b, i: (0, 0)),
                  pl.BlockSpec(w.shape, lambda b, i: (0, 0)),
                  pl.BlockSpec((tm, LANES), lambda b, i: (i, 0)),
                  pl.BlockSpec((tm, LANES), lambda b, i: (i, 0))],
        out_specs=[pl.BlockSpec((1, 3 * DA_HEADS, tm, LANES), lambda b, i: (b, 0, i, 0)),
                   pl.BlockSpec((1, tm, 1024), lambda b, i: (b, i, 0)),
                   pl.BlockSpec((1, tm, 1536), lambda b, i: (b, i, 0))],
        out_shape=[jax.ShapeDtypeStruct((bsz, 3 * DA_HEADS, rows, LANES), BF16),
                   jax.ShapeDtypeStruct((bsz, rows, 1024), F32),
                   jax.ShapeDtypeStruct((bsz, rows, 1536), BF16)],
        compiler_params=_cparams(("parallel", "arbitrary")),
        name="inproj",
    )(x, ctx, mod, g, w, cos, sin)


def _attn_kernel(lam_ref, q_ref, k_ref, v_ref, subln_ref, o_ref, qs_ref, sa_ref, sb_ref, pa_ref, pb_ref, m_ref,
                 acc_ref, *, tq, tk, nchunk, out_scale):
    q = q_ref[0, 0]
    lane = lax.broadcasted_iota(jnp.int32, q.shape, 1)
    zero = jnp.zeros_like(q)
    qs_ref[0:tq, :] = jnp.where(lane < DA_HEAD_DIM, q, zero)
    qs_ref[tq:2 * tq, :] = jnp.where(lane >= DA_HEAD_DIM, q, zero)

    m_ref[...] = jnp.full(m_ref.shape, -1e30, F32)
    acc_ref[...] = jnp.zeros(acc_ref.shape, F32)

    def scores(j, s_ref):
        kj = k_ref[0, 0, pl.ds(pl.multiple_of(j * tk, tk), tk), :]
        s_ref[...] = _dot_nt(qs_ref[...], kj)

    def update(j, s_ref, p_ref):
        vj = v_ref[0, 0, pl.ds(pl.multiple_of(j * tk, tk), tk), :]
        vext = jnp.concatenate([vj, jnp.ones_like(vj)], axis=-1)
        m = m_ref[...]
        m_new = jnp.maximum(m, jnp.broadcast_to(jnp.max(s_ref[...], axis=-1, keepdims=True), m.shape))
        alpha = jnp.exp2(m - m_new)
        m_ref[...] = m_new
        rb = min(ATTN_ROW_BLOCK, 2 * tq)
        for r0 in range(0, 2 * tq, rb):
            rows = slice(r0, r0 + rb)
            mb = m_ref[rows, :]
            for c0 in range(0, tk, LANES):
                p_ref[rows, c0:c0 + LANES] = jnp.exp2((s_ref[rows, c0:c0 + LANES] - mb).astype(BF16))
        acc_ref[...] = jnp.tile(alpha, (1, 2)) * acc_ref[...] + _dot(p_ref[...], vext)

    npairs = (nchunk - 1) // 2
    scores(0, sa_ref)

    def body(i, _):
        scores(2 * i + 1, sb_ref)
        update(2 * i, sa_ref, pa_ref)
        scores(2 * i + 2, sa_ref)
        update(2 * i + 1, sb_ref, pb_ref)
        return 0
    lax.fori_loop(0, npairs, body, 0)
    if nchunk - 1 == 2 * npairs:
        update(nchunk - 1, sa_ref, pa_ref)
    else:
        scores(nchunk - 1, sb_ref)
        update(nchunk - 2, sa_ref, pa_ref)
        update(nchunk - 1, sb_ref, pb_ref)

    r = 1.0 / acc_ref[:, LANES:LANES + 1]
    acc = acc_ref[:, 0:LANES]
    o = acc[0:tq] * r[0:tq] - lam_ref[0, 0] * (acc[tq:2 * tq] * r[tq:2 * tq])
    ms = jnp.mean(o * o, axis=-1, keepdims=True)
    o = o * lax.rsqrt(ms + NORM_EPS) * subln_ref[...] * out_scale
    o_ref[0] = o.astype(o_ref.dtype)


def _attn_call(lam, qkv, subln, t, tq, tk, out_scale):
    bsz, _, rows, _ = qkv.shape
    kern = functools.partial(_attn_kernel, tq=tq, tk=tk, nchunk=rows // tk, out_scale=out_scale)
    return pl.pallas_call(
        kern,
        grid=(bsz, DA_HEADS, t // tq),
        in_specs=[pl.BlockSpec(memory_space=pltpu.SMEM),
                  pl.BlockSpec((1, 1, tq, LANES), lambda b, h, i: (b, h, i, 0)),
                  pl.BlockSpec((1, 1, rows, LANES), lambda b, h, i: (b, DA_HEADS + h, 0, 0)),
                  pl.BlockSpec((1, 1, rows, LANES), lambda b, h, i: (b, 2 * DA_HEADS + h, 0, 0)),
                  pl.BlockSpec((1, LANES), lambda b, h, i: (0, 0))],
        out_specs=pl.BlockSpec((1, tq, LANES), lambda b, h, i: (b, i, h)),
        out_shape=jax.ShapeDtypeStruct((bsz, t, DA_WIDTH), BF16),
        scratch_shapes=[pltpu.VMEM((2 * tq, LANES), BF16),
                        pltpu.VMEM((2 * tq, tk), F32),
                        pltpu.VMEM((2 * tq, tk), F32),
                        pltpu.VMEM((2 * tq, tk), BF16),
                        pltpu.VMEM((2 * tq, tk), BF16),
                        pltpu.VMEM((2 * tq, LANES), F32),
                        pltpu.VMEM((2 * tq, 2 * LANES), F32)],
        compiler_params=_cparams(("parallel", "parallel", "arbitrary")),
        name="attn",
    )(lam, qkv, qkv, qkv, subln)


HGRN_SUB_LEVELS = 3


def _hgrn_consts(c, d):
    nl = int(math.log2(c))
    t = np.arange(c)
    pi = t if d == 0 else c - 1 - t
    pr, pu = pi[:, None], pi[None, :]
    lm = np.zeros(((HGRN_SUB_LEVELS + 1) * c, c), np.float32)
    for lg in range(HGRN_SUB_LEVELS):
        same = (pr >> lg) == (pu >> lg)
        sec = ((pr >> lg) & 1) == 1
        lm[lg * c:(lg + 1) * c] = same & np.where(sec, pu <= pr, pu > pr)
    lm[HGRN_SUB_LEVELS * c:] = pu <= pr
    masks = np.zeros((nl + 1, c, c), np.float32)
    for lg in range(nl):
        masks[lg] = (t[:, None] >> (lg + 1)) == (t[None, :] >> (lg + 1))
    masks[nl] = np.eye(c)
    return jnp.asarray(lm, BF16), jnp.asarray(masks, F32), nl


def _hgrn_kernel(z_ref, i_ref, q_ref, lb_ref, lmat_ref, mask_ref, o_ref, st_ref, *, c, nl, d):
    s = pl.program_id(1)
    heads = range(HG_HEADS)

    @pl.when(s == 0)
    def _():
        st_ref[...] = jnp.zeros(st_ref.shape, F32)

    def col(h):
        return slice(h * LANES, (h + 1) * LANES)

    f = [lb_ref[0, h] + (1.0 - lb_ref[0, h]) * jax.nn.sigmoid(z_ref[0, :, col(h)]) for h in heads]
    lf = [jnp.log(f[h]) for h in heads]
    kk = [1.0 - f[h] for h in heads]
    v = [i_ref[0, :, col(h)] for h in heads]
    q = [q_ref[0, :, col(h)].astype(F32) for h in heads]
    x = []
    for h in heads:
        hi = lf[h].astype(BF16)
        lo = (lf[h] - hi.astype(F32)).astype(BF16)
        x2 = _dot(lmat_ref[...], jnp.concatenate([hi, lo], axis=-1))
        x.append(x2[:, 0:LANES] + x2[:, LANES:2 * LANES])
    cum = [x[h][HGRN_SUB_LEVELS * c:] for h in heads]
    last = c - 1 if d == 0 else 0
    bend = [cum[h][last:last + 1] for h in heads]

    row = lax.broadcasted_iota(jnp.int32, (c, LANES), 0)
    scores = [mask_ref[nl] * _dot_nt(q[h].astype(BF16), kk[h].astype(BF16)) for h in heads]
    for lg in range(nl):
        hs = 1 << lg
        sec = ((row >> lg) & 1) == (1 - d)
        for h in heads:
            if lg < HGRN_SUB_LEVELS:
                xl = x[h][lg * c:(lg + 1) * c]
            else:
                mid = hs - 1 if d == 0 else hs
                c3 = cum[h].reshape(c // (2 * hs), 2 * hs, LANES)
                ref = jnp.broadcast_to(c3[:, mid:mid + 1, :], c3.shape).reshape(c, LANES)
                xl = -jnp.abs(cum[h] - ref)
            g = jnp.exp(xl)
            qt = jnp.where(sec, q[h] * g, 0.0).astype(BF16)
            kt = jnp.where(sec, 0.0, kk[h] * g).astype(BF16)
            scores[h] = scores[h] + mask_ref[lg] * _dot_nt(qt, kt)
    for h in heads:
        st = st_ref[h]
        o = (_dot_nt((q[h] * jnp.exp(cum[h])).astype(BF16), st.astype(BF16))
             + _dot(scores[h].astype(BF16), v[h]))
        kb = (kk[h] * jnp.exp(bend[h] - cum[h])).astype(BF16)
        st_ref[h] = st * jnp.exp(bend[h]) + _dot_tn(v[h], kb)
        o_ref[0, :, col(h)] = o.astype(o_ref.dtype)


def _hgrn_call(z, iqg, lb4, c, d):
    bsz, rows, _ = z.shape
    nb = rows // c - 1
    lmat, masks, nl = _hgrn_consts(c, d)

    def blk(s):
        return jnp.where(s == 0, nb, s - 1 if d == 0 else nb - s)

    kern = functools.partial(_hgrn_kernel, c=c, nl=nl, d=d)
    return pl.pallas_call(
        kern,
        grid=(bsz, nb + 1),
        in_specs=[pl.BlockSpec((1, c, HG_WIDTH), lambda b, s: (b, blk(s), d)),
                  pl.BlockSpec((1, c, HG_WIDTH), lambda b, s: (b, blk(s), 0)),
                  pl.BlockSpec((1, c, HG_WIDTH), lambda b, s: (b, blk(s), 1)),
                  pl.BlockSpec((1, HG_HEADS, 1, LANES), lambda b, s: (d, 0, 0, 0)),
                  pl.BlockSpec(lmat.shape, lambda b, s: (0, 0)),
                  pl.BlockSpec(masks.shape, lambda b, s: (0, 0, 0))],
        out_specs=pl.BlockSpec((1, c, HG_WIDTH), lambda b, s: (b, blk(s), 0)),
        out_shape=jax.ShapeDtypeStruct((bsz, rows, HG_WIDTH), BF16),
        scratch_shapes=[pltpu.VMEM((HG_HEADS, HG_HEAD_DIM, HG_HEAD_DIM), F32)],
        compiler_params=_cparams(("parallel", "arbitrary")),
        name="hgrn_bwd" if d else "hgrn_fwd",
    )(z, iqg, iqg, lb4, lmat, masks)


def _post_kernel(da_ref, of_ref, ob_ref, g_ref, x_ref, mod_ref, hgn_ref, wout_ref, npost_ref, npre_ref,
                 wrh_ref, wrl_ref, x1_ref, h2_ref, aff_ref, *, d):
    b = pl.program_id(0)
    o = of_ref[0].astype(F32) + ob_ref[0].astype(F32)
    parts = []
    for h in range(HG_HEADS):
        oh = o[:, h * LANES:(h + 1) * LANES]
        ms = jnp.mean(oh * oh, axis=-1, keepdims=True)
        parts.append(oh * lax.rsqrt(ms + NORM_EPS) * hgn_ref[...])
    g = g_ref[0].astype(F32)
    ohn = jnp.concatenate(parts, axis=-1) * (g * jax.nn.sigmoid(g))
    mixed = jnp.concatenate([da_ref[0], ohn.astype(BF16)], axis=-1)
    y = _dot(mixed, wout_ref[...])
    yn = y * lax.rsqrt(jnp.mean(y * y, axis=-1, keepdims=True) + NORM_EPS) * npost_ref[...]
    gt1 = mod_ref[pl.ds(b, 1), 2 * d:3 * d]
    sh2 = mod_ref[pl.ds(b, 1), 3 * d:4 * d]
    sc2 = mod_ref[pl.ds(b, 1), 4 * d:5 * d]
    x1 = x_ref[0] + gt1 * yn
    x1_ref[0] = x1
    h2 = x1 * lax.rsqrt(jnp.mean(x1 * x1, axis=-1, keepdims=True) + NORM_EPS) * npre_ref[...]
    h2 = h2 * (1.0 + sc2) + sh2
    h2_ref[0] = h2
    hh = h2.astype(BF16)
    hl = (h2 - hh.astype(F32)).astype(BF16)
    logits = _dot_nt(wrh_ref[...], hh) + (_dot_nt(wrh_ref[...], hl) + _dot_nt(wrl_ref[...], hh))
    mx = jnp.max(logits, axis=0, keepdims=True)
    e = jnp.exp(logits - mx)
    aff_ref[0] = e / jnp.sum(e, axis=0, keepdims=True)


def _post_call(out_da, o_fwd, o_bwd, iqg, x, mod, hg_norm, w_out, npost, npre, wr_hi, wr_lo, tm):
    bsz, t, d = x.shape
    kern = functools.partial(_post_kernel, d=d)
    row = lambda b, i: (0, 0)
    return pl.pallas_call(
        kern,
        grid=(bsz, t // tm),
        in_specs=[pl.BlockSpec((1, tm, DA_WIDTH), lambda b, i: (b, i, 0)),
                  pl.BlockSpec((1, tm, HG_WIDTH), lambda b, i: (b, i, 0)),
                  pl.BlockSpec((1, tm, HG_WIDTH), lambda b, i: (b, i, 0)),
                  pl.BlockSpec((1, tm, HG_WIDTH), lambda b, i: (b, i, 2)),
                  pl.BlockSpec((1, tm, d), lambda b, i: (b, i, 0)),
                  pl.BlockSpec(mod.shape, row),
                  pl.BlockSpec((1, LANES), row),
                  pl.BlockSpec(w_out.shape, row),
                  pl.BlockSpec((1, d), row),
                  pl.BlockSpec((1, d), row),
                  pl.BlockSpec(wr_hi.shape, row),
                  pl.BlockSpec(wr_lo.shape, row)],
        out_specs=[pl.BlockSpec((1, tm, d), lambda b, i: (b, i, 0)),
                   pl.BlockSpec((1, tm, d), lambda b, i: (b, i, 0)),
                   pl.BlockSpec((1, N_EXPERTS, tm), lambda b, i: (b, 0, i))],
        out_shape=[jax.ShapeDtypeStruct((bsz, t, d), F32),
                   jax.ShapeDtypeStruct((bsz, t, d), F32),
                   jax.ShapeDtypeStruct((bsz, N_EXPERTS, t), F32)],
        compiler_params=_cparams(("parallel", "arbitrary")),
        name="post",
    )(out_da, o_fwd, o_bwd, iqg, x, mod, hg_norm, w_out, npost, npre, wr_hi, wr_lo)


TOPK_BLOCK = 256
TOPK_COUNT_BLOCK = 1024


def _topk_kernel(aff_ref, utri_ref, idx_ref, gate_ref, selc_ref, cnt_ref, cnt_smem, acci_ref, accg_ref,
                 *, t, cap):
    ne = N_EXPERTS
    cb = TOPK_BLOCK
    nblk = t // cb
    kb = min(TOPK_COUNT_BLOCK, t)

    def as_float(bits):
        return lax.bitcast_convert_type(bits, F32)

    def count_ge(thr):
        thr_b = jnp.broadcast_to(thr, (ne, kb))

        def body(i, acc):
            a = aff_ref[0, :, pl.ds(pl.multiple_of(i * kb, kb), kb)]
            return acc + jnp.where(a >= thr_b, 1, 0)
        acc = lax.fori_loop(0, t // kb, body, jnp.zeros((ne, kb), jnp.int32))
        return jnp.sum(acc, axis=1, keepdims=True)

    def bit_body(k, tau):
        cand = tau | (jnp.int32(1) << (29 - k))
        return jnp.where(count_ge(as_float(cand)) >= cap, cand, tau)
    tau = lax.fori_loop(0, 30, bit_body, jnp.zeros((ne, 1), jnp.int32))
    tau_f = as_float(tau)
    nxt_f = as_float(tau + 1)

    n_gt = count_ge(nxt_f)
    need = (cap - n_gt).astype(F32)

    utri = utri_ref[...]

    def sel_body(i, carry):
        ceq, csel = carry
        a = aff_ref[0, :, pl.ds(pl.multiple_of(i * cb, cb), cb)]
        gt = a >= nxt_f
        eq = (a >= tau_f) & jnp.logical_not(gt)
        eq_rank = _dot(jnp.where(eq, 1.0, 0.0).astype(BF16), utri) + ceq
        sel = gt | (eq & (eq_rank <= need))
        c_incl = _dot(jnp.where(sel, 1.0, 0.0).astype(BF16), utri) + csel
        selc_ref[:, pl.ds(pl.multiple_of(i * cb, cb), cb)] = jnp.where(sel, c_incl, 0.0)
        cnt_ref[i] = jnp.broadcast_to(csel, (ne, LANES)).astype(jnp.int32)
        return eq_rank[:, cb - 1:cb], c_incl[:, cb - 1:cb]
    zero = jnp.zeros((ne, 1), F32)
    _, c_all = lax.fori_loop(0, nblk, sel_body, (zero, zero))
    cnt_ref[nblk] = jnp.broadcast_to(c_all, (ne, LANES)).astype(jnp.int32)

    def to_smem(i, _):
        blk = cnt_ref[i]
        for e in range(ne):
            cnt_smem[i * ne + e] = blk[e, 0]
        return 0
    lax.fori_loop(0, nblk + 1, to_smem, 0)

    n_pt = cap // LANES
    slot = lax.broadcasted_iota(jnp.int32, (LANES, cb), 0)
    for e in range(ne):
        acci_ref[...] = jnp.zeros(acci_ref.shape, F32)
        accg_ref[...] = jnp.zeros(accg_ref.shape, F32)

        def slot_body(i, _):
            cols = pl.ds(pl.multiple_of(i * cb, cb), cb)
            cc = jnp.broadcast_to(selc_ref[e:e + 1, cols], (LANES, cb))
            aa = jnp.broadcast_to(aff_ref[0, e:e + 1, cols], (LANES, cb))
            tt = (lax.broadcasted_iota(jnp.int32, (LANES, cb), 1) + i * cb).astype(F32)
            first = cnt_smem[i * ne + e]
            last = cnt_smem[(i + 1) * ne + e] - 1

            def tile_body(j, _):
                hit = cc == (slot + (j * LANES + 1)).astype(F32)
                ti = jnp.where(hit, tt, 0.0)
                gi = jnp.where(hit, aa, 0.0)
                for c0 in range(0, cb, LANES):
                    acci_ref[j] = acci_ref[j] + ti[:, c0:c0 + LANES]
                    accg_ref[j] = accg_ref[j] + gi[:, c0:c0 + LANES]
                return 0
            lax.fori_loop(first // LANES, last // LANES + 1, tile_body, 0)
            return 0
        lax.fori_loop(0, nblk, slot_body, 0)
        for j in range(n_pt):
            idx_ref[0, e:e + 1, j * LANES:(j + 1) * LANES] = (
                jnp.sum(acci_ref[j].T, axis=0, keepdims=True).astype(jnp.int32))
            gate_ref[0, e:e + 1, j * LANES:(j + 1) * LANES] = jnp.sum(accg_ref[j].T, axis=0, keepdims=True)


def _topk_call(aff, cap):
    bsz, ne, t = aff.shape
    cb = TOPK_BLOCK
    utri = jnp.asarray(np.triu(np.ones((cb, cb), np.float32)), BF16)
    kern = functools.partial(_topk_kernel, t=t, cap=cap)
    return pl.pallas_call(
        kern,
        grid=(bsz,),
        in_specs=[pl.BlockSpec((1, ne, t), lambda b: (b, 0, 0)),
                  pl.BlockSpec((cb, cb), lambda b: (0, 0))],
        out_specs=[pl.BlockSpec((1, ne, cap), lambda b: (b, 0, 0)),
                   pl.BlockSpec((1, ne, cap), lambda b: (b, 0, 0))],
        out_shape=[jax.ShapeDtypeStruct((bsz, ne, cap), jnp.int32),
                   jax.ShapeDtypeStruct((bsz, ne, cap), F32)],
        scratch_shapes=[pltpu.VMEM((ne, t), F32),
                        pltpu.VMEM((t // cb + 1, ne, LANES), jnp.int32),
                        pltpu.SMEM(((t // cb + 1) * ne,), jnp.int32),
                        pltpu.VMEM((cap // LANES, LANES, LANES), F32),
                        pltpu.VMEM((cap // LANES, LANES, LANES), F32)],
        compiler_params=_cparams(("arbitrary",)),
        name="topk",
    )(aff, utri)


def _moe_kernel(idx_ref, h2_ref, gate_ref, wg_ref, wu_ref, wd_ref, x1_ref, mod_ref, npost_ref, o_ref,
                f_scr, xin, xbf, yacc, xbuf, obuf, gsem, xsem, osem, *, t, cap, ne, nf, zr):
    b = pl.program_id(0)
    e = pl.program_id(1)
    fc = pl.program_id(2)
    nb = pl.num_programs(0)
    flat = b * ne + e
    base = flat * cap
    slot = flat % 2
    last = nb * ne - 1
    nxt = jnp.minimum(flat + 1, last)
    rows_per_fc = cap // nf

    @pl.when((e == 0) & (fc == 0))
    def _():
        def zbody(i, _):
            f_scr[pl.ds(i * zr, zr), :] = jnp.zeros((zr, f_scr.shape[1]), F32)
            return 0
        lax.fori_loop(0, t // zr, zbody, 0)
        yacc[...] = jnp.zeros(yacc.shape, F32)

    def row_copy(step, dst, p):
        tok = idx_ref[step * cap + p]
        return pltpu.make_async_copy(h2_ref.at[step // ne, pl.ds(tok, 1), :],
                                     xin.at[dst, pl.ds(p, 1), :], gsem.at[dst])

    def wait_rows(dst):
        pltpu.make_async_copy(h2_ref.at[0, pl.ds(0, cap), :], xin.at[dst], gsem.at[dst]).wait()

    @pl.when((flat == 0) & (fc == 0))
    def _():
        def gstart(p, _):
            row_copy(flat, slot, p).start()
            return 0
        lax.fori_loop(0, cap, gstart, 0, unroll=8)

    @pl.when(fc == 0)
    def _():
        wait_rows(slot)
        xbf[...] = xin[slot].astype(BF16)

    for p in range(rows_per_fc):
        row_copy(nxt, 1 - slot, fc * rows_per_fc + p).start(priority=p % 2)

    wgb = wg_ref[0].astype(BF16)
    wub = wu_ref[0].astype(BF16)
    wdb = wd_ref[0].astype(BF16)
    half = cap // 2
    first = fc == 0

    def gate_up(rows):
        x = xbf[rows, :]
        return _dot(x, wgb), _dot(x, wub)

    def swiglu(gu):
        g, u = gu
        return (g * jax.nn.sigmoid(g) * u).astype(BF16)

    def down(rows, hmid):
        contrib = _dot(hmid, wdb)
        yacc[rows, :] = jnp.where(first, contrib, yacc[rows, :] + contrib)

    rows_a, rows_b = slice(0, half), slice(half, cap)
    gu_a = gate_up(rows_a)
    h_a = swiglu(gu_a)
    gu_b = gate_up(rows_b)
    down(rows_a, h_a)
    h_b = swiglu(gu_b)
    down(rows_b, h_b)

    @pl.when(fc == nf - 1)
    def _():
        yacc[...] = yacc[...] * gate_ref[0, 0]

        def sbody(g, _):
            p0 = g * SCATTER_GROUP
            toks = [idx_ref[base + p0 + i] for i in range(SCATTER_GROUP)]
            yr = [yacc[pl.ds(p0 + i, 1), :] for i in range(SCATTER_GROUP)]
            fr = [f_scr[pl.ds(toks[i], 1), :] for i in range(SCATTER_GROUP)]
            for i in range(SCATTER_GROUP):
                f_scr[pl.ds(toks[i], 1), :] = fr[i] + yr[i]
            return 0
        lax.fori_loop(0, cap // SCATTER_GROUP, sbody, 0)

    @pl.when((e == ne - 1) & (fc == nf - 1))
    def _():
        d = f_scr.shape[1]
        nblk = t // zr
        gt2 = mod_ref[pl.ds(b, 1), 5 * d:6 * d]

        def x_copy(i, s):
            return pltpu.make_async_copy(x1_ref.at[b, pl.ds(i * zr, zr), :], xbuf.at[s], xsem.at[s])

        def o_copy(i, s):
            return pltpu.make_async_copy(obuf.at[s], o_ref.at[b, pl.ds(i * zr, zr), :], osem.at[s])

        x_copy(0, 0).start()

        def fbody(i, _):
            s = i % 2
            x_copy(i, s).wait()

            @pl.when(i + 1 < nblk)
            def _():
                x_copy(i + 1, 1 - s).start()

            @pl.when(i >= 2)
            def _():
                o_copy(i - 2, s).wait()
            f = f_scr[pl.ds(i * zr, zr), :]
            fn = f * lax.rsqrt(jnp.mean(f * f, axis=-1, keepdims=True) + NORM_EPS) * npost_ref[...]
            obuf[s] = xbuf[s] + gt2 * fn
            o_copy(i, s).start()
            return 0
        lax.fori_loop(0, nblk, fbody, 0)
        o_copy(nblk - 2, nblk % 2).wait()
        o_copy(nblk - 1, (nblk - 1) % 2).wait()

    @pl.when((flat == last) & (fc == nf - 1))
    def _():
        wait_rows(1 - slot)


def _moe_call(idx_flat, h2, gate4, w_gate, w_up, w_down, x1, mod, npost, cap, tf):
    bsz, t, d = h2.shape
    ne, _, ff = w_gate.shape
    nf = ff // tf
    zr = 256
    kern = functools.partial(_moe_kernel, t=t, cap=cap, ne=ne, nf=nf, zr=zr)
    grid_spec = pltpu.PrefetchScalarGridSpec(
        num_scalar_prefetch=1,
        grid=(bsz, ne, nf),
        in_specs=[pl.BlockSpec(memory_space=pl.ANY),
                  pl.BlockSpec((1, 1, cap, 1), lambda b, e, f, idx: (b, e, 0, 0)),
                  pl.BlockSpec((1, d, tf), lambda b, e, f, idx: (e, 0, f)),
                  pl.BlockSpec((1, d, tf), lambda b, e, f, idx: (e, 0, f)),
                  pl.BlockSpec((1, tf, d), lambda b, e, f, idx: (e, f, 0)),
                  pl.BlockSpec(memory_space=pl.ANY),
                  pl.BlockSpec(mod.shape, lambda b, e, f, idx: (0, 0)),
                  pl.BlockSpec((1, d), lambda b, e, f, idx: (0, 0))],
        out_specs=pl.BlockSpec(memory_space=pl.ANY),
        scratch_shapes=[pltpu.VMEM((t, d), F32),
                        pltpu.VMEM((2, cap, d), F32),
                        pltpu.VMEM((cap, d), BF16),
                        pltpu.VMEM((cap, d), F32),
                        pltpu.VMEM((2, zr, d), F32),
                        pltpu.VMEM((2, zr, d), F32),
                        pltpu.SemaphoreType.DMA((2,)),
                        pltpu.SemaphoreType.DMA((2,)),
                        pltpu.SemaphoreType.DMA((2,))],
    )
    return pl.pallas_call(
        kern,
        grid_spec=grid_spec,
        out_shape=jax.ShapeDtypeStruct((bsz, t, d), F32),
        compiler_params=_cparams(("arbitrary", "arbitrary", "arbitrary")),
        name="moe",
    )(idx_flat, h2, gate4, w_gate, w_up, w_down, x1, mod, npost)


def _rope_tables(t, ctx_len):
    half = DA_HEAD_DIM // 2
    inv_freq = 1.0 / (ROPE_THETA ** (jnp.arange(0, half, 2, dtype=F32) / half))
    pos = jnp.arange(t, dtype=jnp.int32)
    r = (pos // GRID_W).astype(F32)
    col = (pos % GRID_W).astype(F32)
    ang_r = r[:, None] * inv_freq
    ang_c = col[:, None] * inv_freq
    ang = jnp.concatenate([ang_r, ang_r, ang_c, ang_c], axis=-1)
    cos = jnp.concatenate([jnp.cos(ang), jnp.ones((ctx_len, DA_HEAD_DIM), F32)], axis=0)
    sin = jnp.concatenate([jnp.sin(ang), jnp.zeros((ctx_len, DA_HEAD_DIM), F32)], axis=0)
    quarter = DA_HEAD_DIM // 4
    sign = jnp.where((jnp.arange(DA_HEAD_DIM) % (2 * quarter)) < quarter, -1.0, 1.0).astype(F32)
    return jnp.tile(cos, (1, 2)), jnp.tile(sin * sign, (1, 2))


def kernel(x, c, ctx, c_ctx, w_ada, b_ada, norm_pre_mix, norm_post_mix, norm_pre_ffn, norm_post_ffn, w_in,
           da_lambda_q1, da_lambda_k1, da_lambda_q2, da_lambda_k2, da_subln, hg_lower_bound, hg_norm, w_out,
           w_router, w_gate, w_up, w_down):
    bsz, t, d = x.shape
    ctx_len = ctx.shape[1]
    assert bsz <= 2 and d == 1024 and w_ada.shape[0] == 1
    cap = CAPACITY_FACTOR * t // N_EXPERTS
    lambda_init = 0.8 - 0.6 * math.exp(-0.3 * 0)

    lam = (jnp.exp(jnp.sum(da_lambda_q1[0].astype(F32) * da_lambda_k1[0].astype(F32)))
           - jnp.exp(jnp.sum(da_lambda_q2[0].astype(F32) * da_lambda_k2[0].astype(F32))) + lambda_init)
    lam = lam.reshape(1, 1)
    lb = jnp.cumsum(jax.nn.softmax(hg_lower_bound.astype(F32), axis=0), axis=0)[0]
    lb4 = lb.reshape(2, HG_HEADS, 1, HG_HEAD_DIM)

    wi = w_in[0]
    scale = DA_HEAD_DIM ** -0.5 * math.log2(math.e)
    wq = wi[:, 0:DA_WIDTH] * scale
    w_all = jnp.concatenate([wq, wi[:, DA_WIDTH:]], axis=1).astype(BF16)
    cos, sin = _rope_tables(t, ctx_len)

    cmat = jnp.concatenate([c, c_ctx[None, :], jnp.zeros((8 - bsz - 1, d), F32)], axis=0)
    mod = _mod_call(cmat, w_ada[0], b_ada[0][None, :])

    qkv, z, iqg = _inproj_call(x, ctx, mod, norm_pre_mix[0][None, :], w_all, cos, sin)

    tq = min(ATTN_TQ, t)
    nk = t + ctx_len
    tk = ATTN_TK if nk % ATTN_TK == 0 else ctx_len
    out_da = _attn_call(lam, qkv, da_subln[0][None, :], t, tq, tk, 1.0 - lambda_init)
    o_fwd = _hgrn_call(z, iqg, lb4, ctx_len, 0)
    o_bwd = _hgrn_call(z, iqg, lb4, ctx_len, 1)

    wr = w_router[0].T
    wr_hi = wr.astype(BF16)
    wr_lo = (wr - wr_hi.astype(F32)).astype(BF16)
    x1, h2, aff = _post_call(out_da, o_fwd, o_bwd, iqg, x, mod, hg_norm[0][None, :], w_out[0].astype(BF16),
                             norm_post_mix[0][None, :], norm_pre_ffn[0][None, :], wr_hi, wr_lo, POST_TM)
    idx, gate = _topk_call(aff, cap)
    return _moe_call(idx.reshape(-1), h2, gate.reshape(bsz, N_EXPERTS, cap, 1), w_gate[0], w_up[0], w_down[0],
                     x1, mod, norm_post_ffn[0][None, :], cap, 256)
```
